```python
import math
import jax, jax.numpy as jnp
from jax import lax
import numpy as np

D_MODEL = 1024
BATCH = 8
SEQ = 8192
DEPTH = 1
DEC_BATCH = 8
DEC_SEQ = 64
PAST_LEN = 4096

CHUNK = 64
BAND_CHUNKS = 8
BAND_ROWS = BAND_CHUNKS * CHUNK
D_RNN = D_MODEL
N_LRU_BLOCKS = 8
LRU_BLOCK = D_RNN // N_LRU_BLOCKS
CONV_W = 4
LRU_C = 8.0
N_HEADS = 8
HEAD_DIM = D_MODEL // N_HEADS
D_ATT = N_HEADS * HEAD_DIM
MAX_REL = 128
ATT_SCALE = HEAD_DIM ** -0.5
NEG_INF = -1e30
D_FF = 4 * D_MODEL
ALPHA = (2 * DEPTH) ** 0.25
BETA = (8 * DEPTH) ** -0.25
LN_EPS = 1e-5
D_IN = 2 * D_RNN + 3 * D_ATT + 2 * D_MODEL
IN_SPLITS = (D_RNN, 2 * D_RNN, 2 * D_RNN + D_ATT, 2 * D_RNN + 2 * D_ATT, 2 * D_RNN + 3 * D_ATT, 2 * D_RNN + 3 * D_ATT + D_MODEL)

kernel_name = 'hawk_chunkband_deepnorm_adaln_step'


def _ln(x, g, b):
    xf = x.astype(jnp.float32)
    mu = jnp.mean(xf, axis=-1, keepdims=True)
    xc = xf - mu
    var = jnp.mean(jnp.square(xc), axis=-1, keepdims=True)
    y = xc * lax.rsqrt(var + LN_EPS) * g.astype(jnp.float32) + b.astype(jnp.float32)
    return y.astype(x.dtype)


def _causal_conv(xr, hist, w, b):
    t = xr.shape[1]
    xp = jnp.concatenate([hist.astype(xr.dtype), xr], axis=1)
    y = b + xp[:, 0:t] * w[0]
    for j in range(1, CONV_W):
        y = y + xp[:, j:j + t] * w[j]
    return y, xp[:, -(CONV_W - 1):]


def _lin_comb(e1, e2):
    a1, b1 = e1
    a2, b2 = e2
    return a1 * a2, a2 * b1 + b2


def _rg_lru(xc, h0, w_rg, b_rg, w_ig, b_ig, lam, reset_first):
    bsz, t, _ = xc.shape
    f32 = jnp.float32
    xf = xc.astype(f32)
    xb = xf.reshape(bsz, t, N_LRU_BLOCKS, LRU_BLOCK)
    r = jax.nn.sigmoid(jnp.einsum('btnc,ncd->btnd', xb, w_rg.astype(f32)) + b_rg.astype(f32)).reshape(bsz, t, D_RNN)
    i = jax.nn.sigmoid(jnp.einsum('btnc,ncd->btnd', xb, w_ig.astype(f32)) + b_ig.astype(f32)).reshape(bsz, t, D_RNN)
    log_a = -LRU_C * r * jax.nn.softplus(-lam.astype(f32))
    a = jnp.exp(log_a)
    mult = jnp.sqrt(-jnp.expm1(2.0 * log_a))
    if reset_first:
        mult = mult.at[:, 0].set(1.0)
        a = a.at[:, 0].set(0.0)
    bterm = mult * (i * xf)
    bterm = bterm.at[:, 0].add(a[:, 0] * h0.astype(f32))
    _, h = lax.associative_scan(_lin_comb, (a, bterm), axis=1)
    return h, h[:, -1]


def _rel_bias(table, rel):
    return table[:, jnp.clip(rel, -MAX_REL, MAX_REL) + MAX_REL].astype(jnp.float32)


def _attend(q, k, v, bias, valid):
    s = jnp.einsum('bqhd,bkhd->bhqk', q, k).astype(jnp.float32) * ATT_SCALE + bias
    if valid is not None:
        s = jnp.where(valid, s, NEG_INF)
    p = jax.nn.softmax(s, axis=-1)
    return jnp.einsum('bhqk,bkhd->bqhd', p.astype(v.dtype), v)


def _prompt_band_attn(q, k, v, table):
    bsz, t = q.shape[0], q.shape[1]
    n_chunks = t // CHUNK
    pad = ((0, 0), (BAND_ROWS, 0), (0, 0), (0, 0))
    kp = jnp.pad(k, pad)
    vp = jnp.pad(v, pad)
    qi = jnp.arange(CHUNK)
    ki = jnp.arange(BAND_ROWS + CHUNK)
    bias = _rel_bias(table, qi[:, None] + BAND_ROWS - ki[None, :])

    def one_chunk(j):
        start = j * CHUNK
        qj = lax.dynamic_slice_in_dim(q, start, CHUNK, axis=1)
        kj = lax.dynamic_slice_in_dim(kp, start, BAND_ROWS + CHUNK, axis=1)
        vj = lax.dynamic_slice_in_dim(vp, start, BAND_ROWS + CHUNK, axis=1)
        valid = (ki >= BAND_ROWS - start)[None, :]
        return _attend(qj, kj, vj, bias, valid)

    o = lax.map(one_chunk, jnp.arange(n_chunks))
    o = jnp.moveaxis(o, 0, 1).reshape(bsz, t, D_ATT)
    rows = min(BAND_ROWS, t)
    return o, k[:, t - rows:], v[:, t - rows:]


def _sample_band_attn(q, k, v, k_past, v_past, table):
    bsz, s = q.shape[0], q.shape[1]
    n_past = k_past.shape[1]
    kk = jnp.concatenate([k_past.astype(k.dtype), k], axis=1)
    vv = jnp.concatenate([v_past.astype(v.dtype), v], axis=1)
    kpos = jnp.arange(n_past + s) - n_past
    bias = _rel_bias(table, jnp.arange(s)[:, None] - kpos[None, :])
    o = _attend(q, kk, vv, bias, None)
    return o.reshape(bsz, s, D_ATT), k, v


def _layer(x, c, conv_hist, h0, k_past, v_past, reset_first, w):
    bsz, t, _ = x.shape
    mod = jax.nn.silu(c) @ w['w_ada'] + w['b_ada']
    sh1, sc1, g1, sh2, sc2, g2 = jnp.split(mod[:, None, :], 6, axis=-1)
    u = x * (1.0 + sc1) + sh1
    xr, gl, q, k, v, ga, gb = jnp.split(u @ w['w_in'], IN_SPLITS, axis=-1)
    xc, conv_new = _causal_conv(xr, conv_hist, w['conv_w'], w['conv_b'])
    h, h_last = _rg_lru(xc, h0, w['w_rg'], w['b_rg'], w['w_ig'], w['b_ig'], w['lru_lambda'], reset_first)
    y_a = h.astype(x.dtype) * jax.nn.gelu(gl)
    heads = (bsz, t, N_HEADS, HEAD_DIM)
    q, k, v = q.reshape(heads), k.reshape(heads), v.reshape(heads)
    if k_past is None:
        y_b, k_new, v_new = _prompt_band_attn(q, k, v, w['rel_bias'])
    else:
        y_b, k_new, v_new = _sample_band_attn(q, k, v, k_past, v_past, w['rel_bias'])
    merged = jax.nn.sigmoid(ga) * y_a + jax.nn.sigmoid(gb) * y_b
    x1 = _ln(ALPHA * x + (1.0 + g1) * (merged @ w['w_out']), w['ln1_g'], w['ln1_b'])
    u2 = x1 * (1.0 + sc2) + sh2
    f = jnp.square(jax.nn.relu(u2 @ w['w_up'] + w['b_up'])) @ w['w_down'] + w['b_down']
    y = _ln(ALPHA * x1 + (1.0 + g2) * f, w['ln2_g'], w['ln2_b'])
    return y, k_new, v_new, conv_new, h_last.astype(x.dtype)


def setup_inputs(seed: int = 0) -> dict:
    key = jax.random.key(seed)
    ks = jax.random.split(key, 32)

    def nrm(k, shape, s):
        return s * jax.random.normal(k, shape, jnp.float32)

    L = DEPTH
    cache_rows = min(BAND_ROWS, PAST_LEN)
    a0 = jax.random.uniform(ks[31], (L, D_RNN), jnp.float32, minval=0.9, maxval=0.999)
    return {
        'x_prompt': nrm(ks[0], (BATCH, SEQ, D_MODEL), 1.0),
        'x_sample': nrm(ks[1], (DEC_BATCH, DEC_SEQ, D_MODEL), 1.0),
        'c_prompt': nrm(ks[2], (BATCH, D_MODEL), 1.0),
        'c_sample': nrm(ks[3], (DEC_BATCH, D_MODEL), 1.0),
        'cache_k': nrm(ks[4], (L, DEC_BATCH, cache_rows, N_HEADS, HEAD_DIM), 1.0),
        'cache_v': nrm(ks[5], (L, DEC_BATCH, cache_rows, N_HEADS, HEAD_DIM), 1.0),
        'state_conv': nrm(ks[6], (L, DEC_BATCH, CONV_W - 1, D_RNN), 1.0),
        'state_lru': nrm(ks[7], (L, DEC_BATCH, D_RNN), 0.3),
        'w_ada': nrm(ks[8], (L, D_MODEL, 6 * D_MODEL), 0.1 * D_MODEL ** -0.5),
        'b_ada': nrm(ks[9], (L, 6 * D_MODEL), 0.01),
        'w_in': nrm(ks[10], (L, D_MODEL, D_IN), D_MODEL ** -0.5),
        'conv_w': nrm(ks[11], (L, CONV_W, D_RNN), CONV_W ** -0.5),
        'conv_b': nrm(ks[12], (L, D_RNN), 0.01),
        'w_rg': nrm(ks[13], (L, N_LRU_BLOCKS, LRU_BLOCK, LRU_BLOCK), LRU_BLOCK ** -0.5),
        'b_rg': nrm(ks[14], (L, N_LRU_BLOCKS, LRU_BLOCK), 0.01),
        'w_ig': nrm(ks[15], (L, N_LRU_BLOCKS, LRU_BLOCK, LRU_BLOCK), LRU_BLOCK ** -0.5),
        'b_ig': nrm(ks[16], (L, N_LRU_BLOCKS, LRU_BLOCK), 0.01),
        'lru_lambda': jnp.log(a0) - jnp.log1p(-a0),
        'rel_bias': nrm(ks[17], (L, N_HEADS, 2 * MAX_REL + 1), 0.2),
        'w_out': nrm(ks[18], (L, D_MODEL, D_MODEL), BETA * D_MODEL ** -0.5),
        'ln1_g': 1.0 + nrm(ks[19], (L, D_MODEL), 0.01),
        'ln1_b': nrm(ks[20], (L, D_MODEL), 0.01),
        'w_up': nrm(ks[21], (L, D_MODEL, D_FF), D_MODEL ** -0.5),
        'b_up': nrm(ks[22], (L, D_FF), 0.01),
        'w_down': nrm(ks[23], (L, D_FF, D_MODEL), BETA * D_FF ** -0.5),
        'b_down': nrm(ks[24], (L, D_MODEL), 0.01),
        'ln2_g': 1.0 + nrm(ks[25], (L, D_MODEL), 0.01),
        'ln2_b': nrm(ks[26], (L, D_MODEL), 0.01),
    }


def reference(x_prompt, x_sample, c_prompt, c_sample, cache_k, cache_v, state_conv, state_lru,
              w_ada, b_ada, w_in, conv_w, conv_b, w_rg, b_rg, w_ig, b_ig, lru_lambda, rel_bias,
              w_out, ln1_g, ln1_b, w_up, b_up, w_down, b_down, ln2_g, ln2_b):
    yp, ys = x_prompt, x_sample
    bp = x_prompt.shape[0]
    kp_l, vp_l, cp_l, hp_l = [], [], [], []
    ks_l, vs_l, cs_l, hs_l = [], [], [], []
    for l in range(DEPTH):
        w = {
            'w_ada': w_ada[l], 'b_ada': b_ada[l], 'w_in': w_in[l],
            'conv_w': conv_w[l], 'conv_b': conv_b[l],
            'w_rg': w_rg[l], 'b_rg': b_rg[l], 'w_ig': w_ig[l], 'b_ig': b_ig[l],
            'lru_lambda': lru_lambda[l], 'rel_bias': rel_bias[l], 'w_out': w_out[l],
            'ln1_g': ln1_g[l], 'ln1_b': ln1_b[l], 'w_up': w_up[l], 'b_up': b_up[l],
            'w_down': w_down[l], 'b_down': b_down[l], 'ln2_g': ln2_g[l], 'ln2_b': ln2_b[l],
        }
        conv0 = jnp.zeros((bp, CONV_W - 1, D_RNN), yp.dtype)
        h0 = jnp.zeros((bp, D_RNN), jnp.float32)
        yp, kp, vp, cp, hp = _layer(yp, c_prompt, conv0, h0, None, None, True, w)
        ys, ksn, vsn, csn, hsn = _layer(ys, c_sample, state_conv[l], state_lru[l], cache_k[l], cache_v[l], False, w)
        kp_l.append(kp); vp_l.append(vp); cp_l.append(cp); hp_l.append(hp)
        ks_l.append(ksn); vs_l.append(vsn); cs_l.append(csn); hs_l.append(hsn)
    return (yp, ys,
            jnp.stack(kp_l), jnp.stack(vp_l), jnp.stack(cp_l), jnp.stack(hp_l),
            jnp.stack(ks_l), jnp.stack(vs_l), jnp.stack(cs_l), jnp.stack(hs_l))
```

```python
import functools
import math

import jax
import jax.numpy as jnp
from jax import lax
from jax.experimental import pallas as pl
from jax.experimental.pallas import tpu as pltpu

D = 1024
CHUNK = 64
BAND_ROWS = 8 * CHUNK
N_HEADS = 8
HEAD_DIM = D // N_HEADS
N_LRU_BLOCKS = 8
LRU_BLOCK = D // N_LRU_BLOCKS
CONV_W = 4
LRU_C = 8.0
MAX_REL = 128
ATT_SCALE = HEAD_DIM ** -0.5
NEG_INF = -1e30
D_FF = 4 * D
D_IN = 7 * D
DEPTH = 1
ALPHA = (2 * DEPTH) ** 0.25
LN_EPS = 1e-5

SUBLANES = 8
VMEM_LIMIT = 56 * 1024 * 1024

COL_XR, COL_GL, COL_Q, COL_K, COL_V, COL_GA, COL_GB = range(7)

_F32 = jnp.float32
_BF16 = jnp.bfloat16


def _resident(shape):
    n = len(shape)
    return pl.BlockSpec(shape, lambda *_: (0,) * n, pipeline_mode=pl.Buffered(1))


def _sigmoid(v):
    return 1.0 / (1.0 + jnp.exp(-v))


def _layer_norm(v, g, b):
    mu = jnp.mean(v, axis=-1, keepdims=True)
    vc = v - mu
    var = jnp.mean(vc * vc, axis=-1, keepdims=True)
    return vc * lax.rsqrt(var + LN_EPS) * g + b


def _ada_kernel(c_ref, w_ref, b_ref, o_ref):
    c = c_ref[...]
    s = (c * _sigmoid(c)).astype(_BF16)
    o_ref[...] = jnp.dot(s, w_ref[...], preferred_element_type=_F32) + b_ref[...]


def _ada(c, w_ada, b_ada):
    n = c.shape[0]
    return pl.pallas_call(
        _ada_kernel,
        out_shape=jax.ShapeDtypeStruct((n, 6 * D), _F32),
        compiler_params=pltpu.CompilerParams(vmem_limit_bytes=VMEM_LIMIT),
        name="ada",
    )(c, w_ada, b_ada)


def _inproj_kernel(x_ref, mod_ref, w_ref, z_ref):
    sh1 = mod_ref[:, 0:D]
    sc1 = mod_ref[:, D:2 * D]
    u = (x_ref[...] * (1.0 + sc1) + sh1).astype(_BF16)
    for j in range(D_IN // D):
        cols = slice(j * D, (j + 1) * D)
        z_ref[:, cols] = jnp.dot(u, w_ref[:, cols], preferred_element_type=_F32).astype(_BF16)


def _inproj(x2, mod3, w_in, tm, tiles_per_batch):
    m = x2.shape[0]
    return pl.pallas_call(
        _inproj_kernel,
        grid=(m // tm,),
        in_specs=[
            pl.BlockSpec((tm, D), lambda i: (i, 0)),
            pl.BlockSpec((None, 1, 6 * D), lambda i: (i // tiles_per_batch, 0, 0)),
            _resident((D, D_IN)),
        ],
        out_specs=pl.BlockSpec((tm, D_IN), lambda i: (i, 0)),
        out_shape=jax.ShapeDtypeStruct((m, D_IN), _BF16),
        compiler_params=pltpu.CompilerParams(
            dimension_semantics=("arbitrary",), vmem_limit_bytes=VMEM_LIMIT),
        name="inproj",
    )(x2, mod3, w_in)


def _lru_kernel(xr_ref, gl_ref, hist_ref, h0_ref, cw_ref, cb_ref, wg_ref, bg_ref, lam_ref,
                ya_ref, conv_ref, hlast_ref,
                xp_scr, a_scr, b_scr, hc_scr, *, tt, reset_first):
    i = pl.program_id(1)
    hist_lo = SUBLANES - (CONV_W - 1)

    @pl.when(i == 0)
    def _():
        xp_scr[hist_lo:SUBLANES, :] = hist_ref[...]
        hc_scr[...] = jnp.broadcast_to(h0_ref[...], (SUBLANES, D))

    xr = xr_ref[...].astype(_F32)
    xp_scr[SUBLANES:SUBLANES + tt, :] = xr
    xc = cb_ref[...] + xp_scr[hist_lo:hist_lo + tt, :] * cw_ref[0:1, :]
    for j in range(1, CONV_W - 1):
        xc = xc + xp_scr[hist_lo + j:hist_lo + j + tt, :] * cw_ref[j:j + 1, :]
    xc = xc + xr * cw_ref[CONV_W - 1:CONV_W, :]

    new_hist = xp_scr[tt + hist_lo:tt + SUBLANES, :]
    xp_scr[hist_lo:SUBLANES, :] = new_hist
    conv_ref[...] = new_hist

    lam = lam_ref[...]
    coef = -LRU_C * (jnp.maximum(-lam, 0.0) + jnp.log1p(jnp.exp(-jnp.abs(lam))))
    if reset_first:
        row = lax.broadcasted_iota(jnp.int32, (tt, LRU_BLOCK), 0)
        first = jnp.logical_and(row == 0, i == 0)
    for n in range(N_LRU_BLOCKS):
        cols = slice(n * LRU_BLOCK, (n + 1) * LRU_BLOCK)
        xcb = xc[:, cols]
        g = jnp.dot(xcb.astype(_BF16), wg_ref[n], preferred_element_type=_F32) + bg_ref[n]
        r = _sigmoid(g[:, :LRU_BLOCK])
        ig = _sigmoid(g[:, LRU_BLOCK:])
        log_a = coef[:, cols] * r
        a = jnp.exp(log_a)
        mult = jnp.sqrt(1.0 - a * a)
        if reset_first:
            a = jnp.where(first, 0.0, a)
            mult = jnp.where(first, 1.0, mult)
        a_scr[:, cols] = a
        b_scr[:, cols] = mult * (ig * xcb)

    srow = lax.broadcasted_iota(jnp.int32, (SUBLANES, D), 0)

    def group(gi, hprev):
        rows = pl.ds(pl.multiple_of(gi * SUBLANES, SUBLANES), SUBLANES)
        a = a_scr[rows, :]
        b = b_scr[rows, :]
        for d in (1, 2, 4):
            keep = srow >= d
            a_sh = jnp.where(keep, pltpu.roll(a, d, 0), 1.0)
            b_sh = jnp.where(keep, pltpu.roll(b, d, 0), 0.0)
            b = a * b_sh + b
            a = a * a_sh
        h = a * hprev + b
        b_scr[rows, :] = h
        return jnp.broadcast_to(h[SUBLANES - 1:SUBLANES, :], (SUBLANES, D))

    hc = lax.fori_loop(0, tt // SUBLANES, group, hc_scr[...], unroll=2)
    hc_scr[...] = hc
    hlast_ref[...] = hc[0:1, :]

    ya_ref[...] = (b_scr[...] * jax.nn.gelu(gl_ref[...].astype(_F32))).astype(_BF16)


def _lru(z, hist, h0, conv_w, conv_b, wg, bg, lam, bsz, t, tt, reset_first):
    nt = t // tt
    kern = functools.partial(_lru_kernel, tt=tt, reset_first=reset_first)
    return pl.pallas_call(
        kern,
        grid=(bsz, nt),
        in_specs=[
            pl.BlockSpec((tt, D), lambda b, i: (b * nt + i, COL_XR)),
            pl.BlockSpec((tt, D), lambda b, i: (b * nt + i, COL_GL)),
            pl.BlockSpec((None, CONV_W - 1, D), lambda b, i: (b, 0, 0)),
            pl.BlockSpec((None, 1, D), lambda b, i: (b, 0, 0)),
            _resident((CONV_W, D)),
            _resident((1, D)),
            _resident((N_LRU_BLOCKS, LRU_BLOCK, 2 * LRU_BLOCK)),
            _resident((N_LRU_BLOCKS, 1, 2 * LRU_BLOCK)),
            _resident((1, D)),
        ],
        out_specs=[
            pl.BlockSpec((tt, D), lambda b, i: (b * nt + i, 0)),
            pl.BlockSpec((None, CONV_W - 1, D), lambda b, i: (b, 0, 0)),
            pl.BlockSpec((None, 1, D), lambda b, i: (b, 0, 0)),
        ],
        out_shape=[
            jax.ShapeDtypeStruct((bsz * t, D), _BF16),
            jax.ShapeDtypeStruct((bsz, CONV_W - 1, D), _F32),
            jax.ShapeDtypeStruct((bsz, 1, D), _F32),
        ],
        scratch_shapes=[
            pltpu.VMEM((SUBLANES + tt, D), _F32),
            pltpu.VMEM((tt, D), _F32),
            pltpu.VMEM((tt, D), _F32),
            pltpu.VMEM((SUBLANES, D), _F32),
        ],
        compiler_params=pltpu.CompilerParams(
            dimension_semantics=("arbitrary", "arbitrary"), vmem_limit_bytes=VMEM_LIMIT),
        name="lru",
    )(z, z, hist, h0, conv_w, conv_b, wg, bg, lam)


def _attn_kernel(q_ref, kp_ref, kc_ref, vp_ref, vc_ref, bias_ref, yb_ref, kwin, vwin,
                 *, tq, sb, mask_first_prev):
    i = pl.program_id(1)
    past = kp_ref.shape[0]
    win = past + sb
    kwin[0:past, :] = kp_ref[...]
    kwin[past:past + tq, :] = kc_ref[...]
    vwin[0:past, :] = vp_ref[...]
    vwin[past:past + tq, :] = vc_ref[...]

    def sub_block(s, carry):
        so = pl.multiple_of(s * sb, sb)
        if mask_first_prev:
            kk = lax.broadcasted_iota(jnp.int32, (sb, win), 1)
            pre = jnp.where(jnp.logical_and(i == 0, kk + so < past), NEG_INF, 0.0)
        for h in range(N_HEADS):
            cols = slice(h * HEAD_DIM, (h + 1) * HEAD_DIM)
            qh = q_ref[pl.ds(so, sb), cols]
            kh = kwin[pl.ds(so, win), cols]
            vh = vwin[pl.ds(so, win), cols]
            sc = lax.dot_general(qh, kh, (((1,), (1,)), ((), ())), preferred_element_type=_F32)
            sc = sc * ATT_SCALE + bias_ref[h]
            if mask_first_prev:
                sc = sc + pre
            m = jnp.max(sc, axis=-1, keepdims=True)
            p = jnp.exp(sc - m)
            l = jnp.sum(p, axis=-1, keepdims=True)
            o = jnp.dot(p.astype(_BF16), vh, preferred_element_type=_F32)
            yb_ref[pl.ds(so, sb), cols] = (o * (1.0 / l)).astype(_BF16)
        return carry

    lax.fori_loop(0, tq // sb, sub_block, 0)


def _attn(q_arr, q_col, kp_arr, kp_map, kc_arr, kc_col, vp_arr, vp_map, vc_arr, vc_col, bias,
          bsz, t, tq, sb, past, mask_first_prev):
    nt = t // tq
    win = past + sb
    kern = functools.partial(_attn_kernel, tq=tq, sb=sb, mask_first_prev=mask_first_prev)
    return pl.pallas_call(
        kern,
        grid=(bsz, nt),
        in_specs=[
            pl.BlockSpec((tq, D), lambda b, i: (b * nt + i, q_col)),
            pl.BlockSpec((past, D), kp_map),
            pl.BlockSpec((tq, D), lambda b, i: (b * nt + i, kc_col)),
            pl.BlockSpec((past, D), vp_map),
            pl.BlockSpec((tq, D), lambda b, i: (b * nt + i, vc_col)),
            _resident((N_HEADS, sb, win)),
        ],
        out_specs=pl.BlockSpec((tq, D), lambda b, i: (b * nt + i, 0)),
        out_shape=jax.ShapeDtypeStruct((bsz * t, D), _BF16),
        scratch_shapes=[
            pltpu.VMEM((past + tq, D), _BF16),
            pltpu.VMEM((past + tq, D), _BF16),
        ],
        compiler_params=pltpu.CompilerParams(
            dimension_semantics=("arbitrary", "arbitrary"), vmem_limit_bytes=VMEM_LIMIT),
        name="attn",
    )(q_arr, kp_arr, kc_arr, vp_arr, vc_arr, bias)


def _final_kernel(x_ref, ya_ref, yb_ref, ga_ref, gb_ref, mod_ref, wo_ref, wu_ref, bu_ref,
                  wd_ref, bd_ref, ln_ref, y_ref):
    g1 = mod_ref[:, 2 * D:3 * D]
    sh2 = mod_ref[:, 3 * D:4 * D]
    sc2 = mod_ref[:, 4 * D:5 * D]
    g2 = mod_ref[:, 5 * D:6 * D]
    merged = (_sigmoid(ga_ref[...].astype(_F32)) * ya_ref[...].astype(_F32)
              + _sigmoid(gb_ref[...].astype(_F32)) * yb_ref[...].astype(_F32))
    o = jnp.dot(merged.astype(_BF16), wo_ref[...], preferred_element_type=_F32)
    x1 = _layer_norm(ALPHA * x_ref[...] + (1.0 + g1) * o, ln_ref[0:1, :], ln_ref[1:2, :])
    u2 = (x1 * (1.0 + sc2) + sh2).astype(_BF16)
    f = bd_ref[...]
    for c in range(D_FF // D):
        cols = slice(c * D, (c + 1) * D)
        hid = jnp.dot(u2, wu_ref[:, cols], preferred_element_type=_F32) + bu_ref[:, cols]
        hid = jnp.square(jnp.maximum(hid, 0.0)).astype(_BF16)
        f = f + jnp.dot(hid, wd_ref[cols, :], preferred_element_type=_F32)
    y_ref[...] = _layer_norm(ALPHA * x1 + (1.0 + g2) * f, ln_ref[2:3, :], ln_ref[3:4, :])


def _final(x2, ya, yb, z, mod3, w_out, w_up, b_up, w_down, b_down, ln, tm, tiles_per_batch):
    m = x2.shape[0]
    row = lambda i: (i, 0)
    return pl.pallas_call(
        _final_kernel,
        grid=(m // tm,),
        in_specs=[
            pl.BlockSpec((tm, D), row),
            pl.BlockSpec((tm, D), row),
            pl.BlockSpec((tm, D), row),
            pl.BlockSpec((tm, D), lambda i: (i, COL_GA)),
            pl.BlockSpec((tm, D), lambda i: (i, COL_GB)),
            pl.BlockSpec((None, 1, 6 * D), lambda i: (i // tiles_per_batch, 0, 0)),
            _resident((D, D)),
            _resident((D, D_FF)),
            _resident((1, D_FF)),
            _resident((D_FF, D)),
            _resident((1, D)),
            _resident((4, D)),
        ],
        out_specs=pl.BlockSpec((tm, D), row),
        out_shape=jax.ShapeDtypeStruct((m, D), _F32),
        compiler_params=pltpu.CompilerParams(
            dimension_semantics=("arbitrary",), vmem_limit_bytes=VMEM_LIMIT),
        name="final",
    )(x2, ya, yb, z, z, mod3, w_out, w_up, b_up, w_down, b_down, ln)


def _rel_bias_window(table, sb, win, past, band):
    qq = jnp.arange(sb)[:, None]
    kk = jnp.arange(win)[None, :]
    rel = jnp.clip(qq + past - kk, -MAX_REL, MAX_REL) + MAX_REL
    bias = table[:, rel].astype(_F32)
    if band:
        dc = kk // CHUNK - qq // CHUNK
        ok = jnp.logical_and(dc >= 0, dc <= BAND_ROWS // CHUNK)
    else:
        ok = jnp.broadcast_to(kk < past + CHUNK, (sb, win))
    return jnp.where(ok[None], bias, NEG_INF)


def _heads(a, bsz, rows):
    return a.reshape(1, bsz, rows, N_HEADS, HEAD_DIM)


@jax.jit
def _forward(x_prompt, x_sample, c_prompt, c_sample, cache_k, cache_v, state_conv, state_lru,
             w_ada, b_ada, w_in, conv_w, conv_b, w_rg, b_rg, w_ig, b_ig, lru_lambda, rel_bias,
             w_out, ln1_g, ln1_b, w_up, b_up, w_down, b_down, ln2_g, ln2_b):
    bp, t, _ = x_prompt.shape
    bs, s, _ = x_sample.shape
    l = 0
    w_in_b = w_in[l].astype(_BF16)
    w_out_b = w_out[l].astype(_BF16)
    w_up_b = w_up[l].astype(_BF16)
    w_down_b = w_down[l].astype(_BF16)
    wg = jnp.concatenate([w_rg[l], w_ig[l]], axis=-1).astype(_BF16)
    bg = jnp.concatenate([b_rg[l], b_ig[l]], axis=-1)[:, None, :]
    lam = lru_lambda[l][None, :]
    cb = conv_b[l][None, :]
    ln = jnp.stack([ln1_g[l], ln1_b[l], ln2_g[l], ln2_b[l]])
    bu = b_up[l][None, :]
    bd = b_down[l][None, :]

    mod = _ada(jnp.concatenate([c_prompt, c_sample], axis=0), w_ada[l].astype(_BF16),
               b_ada[l][None, :])
    mod_p = mod[:bp, None, :]
    mod_s = mod[bp:, None, :]

    tm = 512
    xp2 = x_prompt.reshape(bp * t, D)
    z = _inproj(xp2, mod_p, w_in_b, tm, t // tm)
    ya, conv_p, h_p = _lru(z, jnp.zeros((bp, CONV_W - 1, D), _F32), jnp.zeros((bp, 1, D), _F32),
                           conv_w[l], cb, wg, bg, lam, bp, t, 512, True)
    tq, sb = 512, 128
    nt = t // tq
    bias_p = _rel_bias_window(rel_bias[l], sb, BAND_ROWS + sb, BAND_ROWS, True)
    prev_k = lambda b, i: (b * nt + jnp.maximum(i - 1, 0), COL_K)
    prev_v = lambda b, i: (b * nt + jnp.maximum(i - 1, 0), COL_V)
    yb = _attn(z, COL_Q, z, prev_k, z, COL_K, z, prev_v, z, COL_V, bias_p,
               bp, t, tq, sb, BAND_ROWS, True)
    y_p = _final(xp2, ya, yb, z, mod_p, w_out_b, w_up_b, bu, w_down_b, bd, ln, tm, t // tm)
    z3 = z.reshape(bp, t, D_IN)
    rows = min(BAND_ROWS, t)
    k_p = z3[:, t - rows:, COL_K * D:(COL_K + 1) * D].astype(_F32)
    v_p = z3[:, t - rows:, COL_V * D:(COL_V + 1) * D].astype(_F32)

    xs2 = x_sample.reshape(bs * s, D)
    zs = _inproj(xs2, mod_s, w_in_b, s, 1)
    ya_s, conv_s, h_s = _lru(zs, state_conv[l], state_lru[l][:, None, :], conv_w[l], cb, wg, bg,
                             lam, bs, s, s, False)
    sbs = 2 * CHUNK
    zs3 = zs.reshape(bs, s, D_IN)
    qkv_s = jnp.pad(zs3[:, :, COL_Q * D:(COL_V + 1) * D], ((0, 0), (0, sbs - s), (0, 0)))
    qkv_s = qkv_s.reshape(bs * sbs, 3 * D)
    n_past = cache_k.shape[2]
    ck = cache_k[l].reshape(bs * n_past, D).astype(_BF16)
    cv = cache_v[l].reshape(bs * n_past, D).astype(_BF16)
    bias_s = _rel_bias_window(rel_bias[l], sbs, n_past + sbs, n_past, False)
    cache_map = lambda b, i: (b, 0)
    yb_s = _attn(qkv_s, 0, ck, cache_map, qkv_s, 1, cv, cache_map, qkv_s, 2, bias_s,
                 bs, sbs, sbs, sbs, n_past, False)
    yb_s = yb_s.reshape(bs, sbs, D)[:, :s].reshape(bs * s, D)
    y_s = _final(xs2, ya_s, yb_s, zs, mod_s, w_out_b, w_up_b, bu, w_down_b, bd, ln, s, 1)
    k_s = zs3[:, :, COL_K * D:(COL_K + 1) * D].astype(_F32)
    v_s = zs3[:, :, COL_V * D:(COL_V + 1) * D].astype(_F32)

    return (y_p.reshape(bp, t, D), y_s.reshape(bs, s, D),
            _heads(k_p, bp, rows), _heads(v_p, bp, rows), conv_p[None], h_p.reshape(1, bp, D),
            _heads(k_s, bs, s), _heads(v_s, bs, s), conv_s[None], h_s.reshape(1, bs, D))


def kernel(x_prompt, x_sample, c_prompt, c_sample, cache_k, cache_v, state_conv, state_lru, w_ada, b_ada, w_in, conv_w, conv_b, w_rg, b_rg, w_ig, b_ig, lru_lambda, rel_bias, w_out, ln1_g, ln1_b, w_up, b_up, w_down, b_down, ln2_g, ln2_b):
    return _forward(x_prompt, x_sample, c_prompt, c_sample, cache_k, cache_v, state_conv,
                    state_lru, w_ada, b_ada, w_in, conv_w, conv_b, w_rg, b_rg, w_ig, b_ig,
                    lru_lambda, rel_bias, w_out, ln1_g, ln1_b, w_up, b_up, w_down, b_down,
                    ln2_g, ln2_b)
```

```python
import functools
import math

import jax
import jax.numpy as jnp
from jax import lax
from jax.experimental import pallas as pl
from jax.experimental.pallas import tpu as pltpu

D = 1024
CHUNK = 64
BAND_ROWS = 8 * CHUNK
N_HEADS = 8
HEAD_DIM = D // N_HEADS
N_LRU_BLOCKS = 8
LRU_BLOCK = D // N_LRU_BLOCKS
CONV_W = 4
LRU_C = 8.0
MAX_REL = 128
ATT_SCALE = HEAD_DIM ** -0.5
NEG_INF = -1e30
D_FF = 4 * D
D_IN = 7 * D
DEPTH = 1
ALPHA = (2 * DEPTH) ** 0.25
LN_EPS = 1e-5

LOG2E = math.log2(math.e)
Q_SCALE = ATT_SCALE * LOG2E
ROLL_W = 768
SC_AHEAD = 2
SC_SLOTS = SC_AHEAD + 1

SUBLANES = 8
VMEM_LIMIT = 56 * 1024 * 1024

COL_XR, COL_GL, COL_Q, COL_K, COL_V, COL_GA, COL_GB = range(7)

_F32 = jnp.float32
_BF16 = jnp.bfloat16


def _resident(shape):
    n = len(shape)
    return pl.BlockSpec(shape, lambda *_: (0,) * n, pipeline_mode=pl.Buffered(1))


def _sigmoid(v):
    return 0.5 * jnp.tanh(0.5 * v) + 0.5


def _layer_norm(v, g, b):
    mu = jnp.mean(v, axis=-1, keepdims=True)
    vc = v - mu
    var = jnp.mean(vc * vc, axis=-1, keepdims=True)
    return vc * lax.rsqrt(var + LN_EPS) * g + b


def _ada_kernel(c_ref, w_ref, b_ref, o_ref):
    c = c_ref[...]
    s = (c * _sigmoid(c)).astype(_BF16)
    o_ref[...] = jnp.dot(s, w_ref[...], preferred_element_type=_F32) + b_ref[...]


def _ada(c, w_ada, b_ada):
    n = c.shape[0]
    return pl.pallas_call(
        _ada_kernel,
        out_shape=jax.ShapeDtypeStruct((n, 6 * D), _F32),
        compiler_params=pltpu.CompilerParams(vmem_limit_bytes=VMEM_LIMIT),
        name="ada",
    )(c, w_ada, b_ada)


def _inproj_kernel(x_ref, mod_ref, w_ref, z_ref):
    sh1 = mod_ref[:, 0:D]
    sc1 = mod_ref[:, D:2 * D]
    u = (x_ref[...] * (1.0 + sc1) + sh1).astype(_BF16)
    for j in range(D_IN // D):
        cols = slice(j * D, (j + 1) * D)
        zj = jnp.dot(u, w_ref[:, cols], preferred_element_type=_F32)
        if j == COL_Q:
            zj = zj * Q_SCALE
        z_ref[:, cols] = zj.astype(_BF16)


def _inproj(x2, mod3, w_in, tm, tiles_per_batch):
    m = x2.shape[0]
    return pl.pallas_call(
        _inproj_kernel,
        grid=(m // tm,),
        in_specs=[
            pl.BlockSpec((tm, D), lambda i: (i, 0)),
            pl.BlockSpec((None, 1, 6 * D), lambda i: (i // tiles_per_batch, 0, 0)),
            _resident((D, D_IN)),
        ],
        out_specs=pl.BlockSpec((tm, D_IN), lambda i: (i, 0)),
        out_shape=jax.ShapeDtypeStruct((m, D_IN), _BF16),
        compiler_params=pltpu.CompilerParams(
            dimension_semantics=("arbitrary",), vmem_limit_bytes=VMEM_LIMIT),
        name="inproj",
    )(x2, mod3, w_in)


def _lru_kernel(xr_ref, gl_ref, hist_ref, h0_ref, cw_ref, cb_ref, wg_ref, bg_ref, lam_ref,
                ya_ref, conv_ref, hlast_ref,
                xp_scr, a_scr, b_scr, hc_scr, *, tt, reset_first):
    i = pl.program_id(1)
    hist_lo = SUBLANES - (CONV_W - 1)

    @pl.when(i == 0)
    def _():
        xp_scr[hist_lo:SUBLANES, :] = hist_ref[...]
        hc_scr[...] = jnp.broadcast_to(h0_ref[...], (SUBLANES, D))

    xr = xr_ref[...].astype(_F32)
    xp_scr[SUBLANES:SUBLANES + tt, :] = xr
    xc = cb_ref[...] + xp_scr[hist_lo:hist_lo + tt, :] * cw_ref[0:1, :]
    for j in range(1, CONV_W - 1):
        xc = xc + xp_scr[hist_lo + j:hist_lo + j + tt, :] * cw_ref[j:j + 1, :]
    xc = xc + xr * cw_ref[CONV_W - 1:CONV_W, :]

    new_hist = xp_scr[tt + hist_lo:tt + SUBLANES, :]
    xp_scr[hist_lo:SUBLANES, :] = new_hist
    conv_ref[...] = new_hist

    lam = lam_ref[...]
    coef = -LRU_C * (jnp.maximum(-lam, 0.0) + jnp.log1p(jnp.exp(-jnp.abs(lam))))
    if reset_first:
        row = lax.broadcasted_iota(jnp.int32, (SUBLANES, LRU_BLOCK), 0)
        first = jnp.logical_and(row == 0, i == 0)

        def at_stream_start(v, value):
            return jnp.concatenate([jnp.where(first, value, v[:SUBLANES]), v[SUBLANES:]], axis=0)
    for n in range(N_LRU_BLOCKS):
        cols = slice(n * LRU_BLOCK, (n + 1) * LRU_BLOCK)
        xcb = xc[:, cols]
        g = jnp.dot(xcb.astype(_BF16), wg_ref[n], preferred_element_type=_F32) + bg_ref[n]
        r = _sigmoid(g[:, :LRU_BLOCK])
        ig = _sigmoid(g[:, LRU_BLOCK:])
        log_a = coef[:, cols] * r
        a = jnp.exp(log_a)
        mult = jnp.sqrt(1.0 - a * a)
        if reset_first:
            a = at_stream_start(a, 0.0)
            mult = at_stream_start(mult, 1.0)
        a_scr[:, cols] = a
        b_scr[:, cols] = mult * (ig * xcb)

    srow = lax.broadcasted_iota(jnp.int32, (SUBLANES, D), 0)

    def group(gi, hprev):
        rows = pl.ds(pl.multiple_of(gi * SUBLANES, SUBLANES), SUBLANES)
        a = a_scr[rows, :]
        b = b_scr[rows, :]
        for d in (1, 2, 4):
            keep = srow >= d
            a_sh = jnp.where(keep, pltpu.roll(a, d, 0), 1.0)
            b_sh = jnp.where(keep, pltpu.roll(b, d, 0), 0.0)
            b = a * b_sh + b
            a = a * a_sh
        h = a * hprev + b
        b_scr[rows, :] = h
        return jnp.broadcast_to(h[SUBLANES - 1:SUBLANES, :], (SUBLANES, D))

    hc = lax.fori_loop(0, tt // SUBLANES, group, hc_scr[...], unroll=2)
    hc_scr[...] = hc
    hlast_ref[...] = hc[0:1, :]

    ya_ref[...] = (b_scr[...] * jax.nn.gelu(gl_ref[...].astype(_F32))).astype(_BF16)


def _lru(z, hist, h0, conv_w, conv_b, wg, bg, lam, bsz, t, tt, reset_first):
    nt = t // tt
    kern = functools.partial(_lru_kernel, tt=tt, reset_first=reset_first)
    return pl.pallas_call(
        kern,
        grid=(bsz, nt),
        in_specs=[
            pl.BlockSpec((tt, D), lambda b, i: (b * nt + i, COL_XR)),
            pl.BlockSpec((tt, D), lambda b, i: (b * nt + i, COL_GL)),
            pl.BlockSpec((None, CONV_W - 1, D), lambda b, i: (b, 0, 0)),
            pl.BlockSpec((None, 1, D), lambda b, i: (b, 0, 0)),
            _resident((CONV_W, D)),
            _resident((1, D)),
            _resident((N_LRU_BLOCKS, LRU_BLOCK, 2 * LRU_BLOCK)),
            _resident((N_LRU_BLOCKS, 1, 2 * LRU_BLOCK)),
            _resident((1, D)),
        ],
        out_specs=[
            pl.BlockSpec((tt, D), lambda b, i: (b * nt + i, 0)),
            pl.BlockSpec((None, CONV_W - 1, D), lambda b, i: (b, 0, 0)),
            pl.BlockSpec((None, 1, D), lambda b, i: (b, 0, 0)),
        ],
        out_shape=[
            jax.ShapeDtypeStruct((bsz * t, D), _BF16),
            jax.ShapeDtypeStruct((bsz, CONV_W - 1, D), _F32),
            jax.ShapeDtypeStruct((bsz, 1, D), _F32),
        ],
        scratch_shapes=[
            pltpu.VMEM((SUBLANES + tt, D), _F32),
            pltpu.VMEM((tt, D), _F32),
            pltpu.VMEM((tt, D), _F32),
            pltpu.VMEM((SUBLANES, D), _F32),
        ],
        compiler_params=pltpu.CompilerParams(
            dimension_semantics=("arbitrary", "arbitrary"), vmem_limit_bytes=VMEM_LIMIT),
        name="lru",
    )(z, z, hist, h0, conv_w, conv_b, wg, bg, lam)


def _attn_kernel(q_ref, kp_ref, kc_ref, vp_ref, vc_ref, row0_ref, yb_ref, bias_scr, sc_scr,
                 *, tq, sb, band):
    i = pl.program_id(1)
    past = kp_ref.shape[0]
    win = past + sb
    nsub = tq // sb
    general = nsub if band else 0

    @pl.when(jnp.logical_and(pl.program_id(0) == 0, i == 0))
    def _build_bias():
        qq = lax.broadcasted_iota(jnp.int32, (sb, win), 0)
        kk = lax.broadcasted_iota(jnp.int32, (sb, win), 1)
        if band:
            dc = (kk // CHUNK) - (qq // CHUNK)
            ok = jnp.logical_and(dc >= 0, dc <= BAND_ROWS // CHUNK)
        else:
            ok = kk < past + CHUNK
        for h in range(N_HEADS):
            base = jnp.broadcast_to(row0_ref[h:h + 1, :], (sb, ROLL_W))
            toep = pltpu.roll(base, 0, 1, stride=1, stride_axis=0)[:, :win] * LOG2E
            gen = jnp.where(ok, toep, NEG_INF)
            bias_scr[general, h] = gen
            if band:
                for s in range(nsub):
                    bias_scr[s, h] = jnp.where(kk + s * sb >= past, gen, NEG_INF)

    def rows(prev_ref, cur_ref, so, cols):
        parts = []
        if so < past:
            parts.append(prev_ref[so:past, cols])
        parts.append(cur_ref[max(so - past, 0):so + sb, cols])
        return parts[0] if len(parts) == 1 else jnp.concatenate(parts, axis=0)

    def scores(n, s, h):
        cols = slice(h * HEAD_DIM, (h + 1) * HEAD_DIM)
        qh = q_ref[s * sb:(s + 1) * sb, cols]
        kh = rows(kp_ref, kc_ref, s * sb, cols)
        sc = lax.dot_general(qh, kh, (((1,), (1,)), ((), ())), preferred_element_type=_F32)
        sc_scr[n % SC_SLOTS] = sc + bias_scr[jnp.where(i == 0, s, general) if band else 0, h]

    def finish(n, s, h):
        cols = slice(h * HEAD_DIM, (h + 1) * HEAD_DIM)
        sc = sc_scr[n % SC_SLOTS]
        p = jnp.exp2(sc - jnp.max(sc, axis=-1, keepdims=True))
        l = jnp.sum(p, axis=-1, keepdims=True)
        vh = rows(vp_ref, vc_ref, s * sb, cols)
        o = jnp.dot(p.astype(_BF16), vh, preferred_element_type=_F32)
        yb_ref[s * sb:(s + 1) * sb, cols] = (o * (1.0 / l)).astype(_BF16)

    pairs = [(s, h) for s in range(nsub) for h in range(N_HEADS)]
    for n in range(min(SC_AHEAD, len(pairs))):
        scores(n, *pairs[n])
    for n, (s, h) in enumerate(pairs):
        if n + SC_AHEAD < len(pairs):
            scores(n + SC_AHEAD, *pairs[n + SC_AHEAD])
        finish(n, s, h)


def _attn(q_arr, q_col, kp_arr, kp_map, kc_arr, kc_col, vp_arr, vp_map, vc_arr, vc_col, row0,
          bsz, t, tq, sb, past, band):
    nt = t // tq
    win = past + sb
    n_bias = tq // sb + 1 if band else 1
    kern = functools.partial(_attn_kernel, tq=tq, sb=sb, band=band)
    return pl.pallas_call(
        kern,
        grid=(bsz, nt),
        in_specs=[
            pl.BlockSpec((tq, D), lambda b, i: (b * nt + i, q_col)),
            pl.BlockSpec((past, D), kp_map),
            pl.BlockSpec((tq, D), lambda b, i: (b * nt + i, kc_col)),
            pl.BlockSpec((past, D), vp_map),
            pl.BlockSpec((tq, D), lambda b, i: (b * nt + i, vc_col)),
            _resident((N_HEADS, ROLL_W)),
        ],
        out_specs=pl.BlockSpec((tq, D), lambda b, i: (b * nt + i, 0)),
        out_shape=jax.ShapeDtypeStruct((bsz * t, D), _BF16),
        scratch_shapes=[
            pltpu.VMEM((n_bias, N_HEADS, sb, win), _F32),
            pltpu.VMEM((SC_SLOTS, sb, win), _F32),
        ],
        compiler_params=pltpu.CompilerParams(
            dimension_semantics=("arbitrary", "arbitrary"), vmem_limit_bytes=VMEM_LIMIT),
        name="attn",
    )(q_arr, kp_arr, kc_arr, vp_arr, vc_arr, row0)


def _final_kernel(x_ref, ya_ref, yb_ref, ga_ref, gb_ref, mod_ref, wo_ref, wu_ref, bu_ref,
                  wd_ref, bd_ref, ln_ref, y_ref):
    g1 = mod_ref[:, 2 * D:3 * D]
    sh2 = mod_ref[:, 3 * D:4 * D]
    sc2 = mod_ref[:, 4 * D:5 * D]
    g2 = mod_ref[:, 5 * D:6 * D]
    merged = (_sigmoid(ga_ref[...].astype(_F32)) * ya_ref[...].astype(_F32)
              + _sigmoid(gb_ref[...].astype(_F32)) * yb_ref[...].astype(_F32))
    o = jnp.dot(merged.astype(_BF16), wo_ref[...], preferred_element_type=_F32)
    x1 = _layer_norm(ALPHA * x_ref[...] + (1.0 + g1) * o, ln_ref[0:1, :], ln_ref[1:2, :])
    u2 = (x1 * (1.0 + sc2) + sh2).astype(_BF16)
    f = bd_ref[...]
    for c in range(D_FF // D):
        cols = slice(c * D, (c + 1) * D)
        hid = jnp.dot(u2, wu_ref[:, cols], preferred_element_type=_F32) + bu_ref[:, cols]
        hid = jnp.square(jnp.maximum(hid, 0.0)).astype(_BF16)
        f = f + jnp.dot(hid, wd_ref[cols, :], preferred_element_type=_F32)
    y_ref[...] = _layer_norm(ALPHA * x1 + (1.0 + g2) * f, ln_ref[2:3, :], ln_ref[3:4, :])


def _final(x2, ya, yb, z, mod3, w_out, w_up, b_up, w_down, b_down, ln, tm, tiles_per_batch):
    m = x2.shape[0]
    row = lambda i: (i, 0)
    return pl.pallas_call(
        _final_kernel,
        grid=(m // tm,),
        in_specs=[
            pl.BlockSpec((tm, D), row),
            pl.BlockSpec((tm, D), row),
            pl.BlockSpec((tm, D), row),
            pl.BlockSpec((tm, D), lambda i: (i, COL_GA)),
            pl.BlockSpec((tm, D), lambda i: (i, COL_GB)),
            pl.BlockSpec((None, 1, 6 * D), lambda i: (i // tiles_per_batch, 0, 0)),
            _resident((D, D)),
            _resident((D, D_FF)),
            _resident((1, D_FF)),
            _resident((D_FF, D)),
            _resident((1, D)),
            _resident((4, D)),
        ],
        out_specs=pl.BlockSpec((tm, D), row),
        out_shape=jax.ShapeDtypeStruct((m, D), _F32),
        compiler_params=pltpu.CompilerParams(
            dimension_semantics=("arbitrary",), vmem_limit_bytes=VMEM_LIMIT),
        name="final",
    )(x2, ya, yb, z, z, mod3, w_out, w_up, b_up, w_down, b_down, ln)


def _rel_bias_row0(table, past):
    far = jnp.broadcast_to(table[:, 2 * MAX_REL:], (N_HEADS, past - MAX_REL + 1))
    near = table[:, 2 * MAX_REL - 1::-1]
    tail = jnp.broadcast_to(table[:, 2 * MAX_REL:],
                            (N_HEADS, ROLL_W - (past + MAX_REL + 1)))
    return jnp.concatenate([far, near, tail], axis=1).astype(_F32)


def _heads(a, bsz, rows):
    return a.reshape(1, bsz, rows, N_HEADS, HEAD_DIM)


@jax.jit
def _forward(x_prompt, x_sample, c_prompt, c_sample, cache_k, cache_v, state_conv, state_lru,
             w_ada, b_ada, w_in, conv_w, conv_b, w_rg, b_rg, w_ig, b_ig, lru_lambda, rel_bias,
             w_out, ln1_g, ln1_b, w_up, b_up, w_down, b_down, ln2_g, ln2_b):
    bp, t, _ = x_prompt.shape
    bs, s, _ = x_sample.shape
    l = 0
    w_in_b = w_in[l].astype(_BF16)
    w_out_b = w_out[l].astype(_BF16)
    w_up_b = w_up[l].astype(_BF16)
    w_down_b = w_down[l].astype(_BF16)
    wg = jnp.concatenate([w_rg[l], w_ig[l]], axis=-1).astype(_BF16)
    bg = jnp.concatenate([b_rg[l], b_ig[l]], axis=-1)[:, None, :]
    lam = lru_lambda[l][None, :]
    cb = conv_b[l][None, :]
    ln = jnp.stack([ln1_g[l], ln1_b[l], ln2_g[l], ln2_b[l]])
    bu = b_up[l][None, :]
    bd = b_down[l][None, :]

    mod = _ada(jnp.concatenate([c_prompt, c_sample], axis=0), w_ada[l].astype(_BF16),
               b_ada[l][None, :])
    mod_p = mod[:bp, None, :]
    mod_s = mod[bp:, None, :]

    tm = 512
    xp2 = x_prompt.reshape(bp * t, D)
    z = _inproj(xp2, mod_p, w_in_b, tm, t // tm)
    ya, conv_p, h_p = _lru(z, jnp.zeros((bp, CONV_W - 1, D), _F32), jnp.zeros((bp, 1, D), _F32),
                           conv_w[l], cb, wg, bg, lam, bp, t, 512, True)
    tq, sb = 512, 128
    nt = t // tq
    row0 = _rel_bias_row0(rel_bias[l], BAND_ROWS)
    prev_k = lambda b, i: (b * nt + jnp.maximum(i - 1, 0), COL_K)
    prev_v = lambda b, i: (b * nt + jnp.maximum(i - 1, 0), COL_V)
    yb = _attn(z, COL_Q, z, prev_k, z, COL_K, z, prev_v, z, COL_V, row0,
               bp, t, tq, sb, BAND_ROWS, True)
    y_p = _final(xp2, ya, yb, z, mod_p, w_out_b, w_up_b, bu, w_down_b, bd, ln, tm, t // tm)
    z3 = z.reshape(bp, t, D_IN)
    rows = min(BAND_ROWS, t)
    k_p = z3[:, t - rows:, COL_K * D:(COL_K + 1) * D].astype(_F32)
    v_p = z3[:, t - rows:, COL_V * D:(COL_V + 1) * D].astype(_F32)

    xs2 = x_sample.reshape(bs * s, D)
    zs = _inproj(xs2, mod_s, w_in_b, s, 1)
    ya_s, conv_s, h_s = _lru(zs, state_conv[l], state_lru[l][:, None, :], conv_w[l], cb, wg, bg,
                             lam, bs, s, s, False)
    sbs = 2 * CHUNK
    zs3 = zs.reshape(bs, s, D_IN)
    qkv_s = jnp.pad(zs3[:, :, COL_Q * D:(COL_V + 1) * D], ((0, 0), (0, sbs - s), (0, 0)))
    qkv_s = qkv_s.reshape(bs * sbs, 3 * D)
    n_past = cache_k.shape[2]
    ck = cache_k[l].reshape(bs * n_past, D).astype(_BF16)
    cv = cache_v[l].reshape(bs * n_past, D).astype(_BF16)
    cache_map = lambda b, i: (b, 0)
    yb_s = _attn(qkv_s, 0, ck, cache_map, qkv_s, 1, cv, cache_map, qkv_s, 2,
                 _rel_bias_row0(rel_bias[l], n_past), bs, sbs, sbs, sbs, n_past, False)
    yb_s = yb_s.reshape(bs, sbs, D)[:, :s].reshape(bs * s, D)
    y_s = _final(xs2, ya_s, yb_s, zs, mod_s, w_out_b, w_up_b, bu, w_down_b, bd, ln, s, 1)
    k_s = zs3[:, :, COL_K * D:(COL_K + 1) * D].astype(_F32)
    v_s = zs3[:, :, COL_V * D:(COL_V + 1) * D].astype(_F32)

    return (y_p.reshape(bp, t, D), y_s.reshape(bs, s, D),
            _heads(k_p, bp, rows), _heads(v_p, bp, rows), conv_p[None], h_p.reshape(1, bp, D),
            _heads(k_s, bs, s), _heads(v_s, bs, s), conv_s[None], h_s.reshape(1, bs, D))


def kernel(x_prompt, x_sample, c_prompt, c_sample, cache_k, cache_v, state_conv, state_lru, w_ada, b_ada, w_in, conv_w, conv_b, w_rg, b_rg, w_ig, b_ig, lru_lambda, rel_bias, w_out, ln1_g, ln1_b, w_up, b_up, w_down, b_down, ln2_g, ln2_b):
    return _forward(x_prompt, x_sample, c_prompt, c_sample, cache_k, cache_v, state_conv,
                    state_lru, w_ada, b_ada, w_in, conv_w, conv_b, w_rg, b_rg, w_ig, b_ig,
                    lru_lambda, rel_bias, w_out, ln1_g, ln1_b, w_up, b_up, w_down, b_down,
                    ln2_g, ln2_b)
```

```python
import functools
import math

import jax
import jax.numpy as jnp
from jax import lax
from jax.experimental import pallas as pl
from jax.experimental.pallas import tpu as pltpu

D = 1024
CHUNK = 64
BAND_ROWS = 8 * CHUNK
N_HEADS = 8
HEAD_DIM = D // N_HEADS
N_LRU_BLOCKS = 8
LRU_BLOCK = D // N_LRU_BLOCKS
CONV_W = 4
LRU_C = 8.0
MAX_REL = 128
ATT_SCALE = HEAD_DIM ** -0.5
NEG_INF = -1e30
D_FF = 4 * D
D_IN = 7 * D
DEPTH = 1
ALPHA = (2 * DEPTH) ** 0.25
LN_EPS = 1e-5

LOG2E = math.log2(math.e)
Q_SCALE = ATT_SCALE * LOG2E
ROLL_W = 768
SC_AHEAD = 2
SC_SLOTS = SC_AHEAD + 1

GELU_C1 = math.sqrt(2.0 / math.pi)
GELU_C2 = GELU_C1 * 0.044715
SQRT_FLOOR = 1e-30

SUBLANES = 8
VMEM_LIMIT = 56 * 1024 * 1024

COL_XR, COL_GL, COL_Q, COL_K, COL_V, COL_GA, COL_GB = range(7)

_F32 = jnp.float32
_BF16 = jnp.bfloat16


def _resident(shape):
    n = len(shape)
    return pl.BlockSpec(shape, lambda *_: (0,) * n, pipeline_mode=pl.Buffered(1))


def _sigmoid(v):
    return 0.5 * jnp.tanh(0.5 * v) + 0.5


def _layer_norm(v, g, b):
    mu = jnp.mean(v, axis=-1, keepdims=True)
    vc = v - mu
    var = jnp.mean(vc * vc, axis=-1, keepdims=True)
    return vc * lax.rsqrt(var + LN_EPS) * g + b


def _ada_kernel(c_ref, w_ref, b_ref, o_ref):
    c = c_ref[...]
    s = (c * _sigmoid(c)).astype(_BF16)
    o_ref[...] = jnp.dot(s, w_ref[...], preferred_element_type=_F32) + b_ref[...]


def _ada(c, w_ada, b_ada):
    n = c.shape[0]
    return pl.pallas_call(
        _ada_kernel,
        out_shape=jax.ShapeDtypeStruct((n, 6 * D), _F32),
        compiler_params=pltpu.CompilerParams(vmem_limit_bytes=VMEM_LIMIT),
        name="ada",
    )(c, w_ada, b_ada)


def _inproj_kernel(x_ref, mod_ref, w_ref, z_ref):
    sh1 = mod_ref[:, 0:D]
    sc1 = mod_ref[:, D:2 * D]
    u = (x_ref[...] * (1.0 + sc1) + sh1).astype(_BF16)
    for j in range(D_IN // D):
        cols = slice(j * D, (j + 1) * D)
        zj = jnp.dot(u, w_ref[:, cols], preferred_element_type=_F32)
        if j == COL_Q:
            zj = zj * Q_SCALE
        z_ref[:, cols] = zj.astype(_BF16)


def _inproj(x2, mod3, w_in, tm, tiles_per_batch):
    m = x2.shape[0]
    return pl.pallas_call(
        _inproj_kernel,
        grid=(m // tm,),
        in_specs=[
            pl.BlockSpec((tm, D), lambda i: (i, 0)),
            pl.BlockSpec((None, 1, 6 * D), lambda i: (i // tiles_per_batch, 0, 0)),
            _resident((D, D_IN)),
        ],
        out_specs=pl.BlockSpec((tm, D_IN), lambda i: (i, 0)),
        out_shape=jax.ShapeDtypeStruct((m, D_IN), _BF16),
        compiler_params=pltpu.CompilerParams(
            dimension_semantics=("arbitrary",), vmem_limit_bytes=VMEM_LIMIT),
        name="inproj",
    )(x2, mod3, w_in)


def _seg_pitch(seg):
    return seg if (seg // SUBLANES) % 2 == 1 else seg + SUBLANES


def _lru_kernel(xr_ref, gl_ref, hist_ref, h0_ref, cw_ref, cb_ref, wg_ref, bg_ref, lam_ref,
                ya_ref, conv_ref, hlast_ref, stage, hist_scr, hc_scr, *, tt, reset_first):
    i = pl.program_id(1)
    seg = tt // SUBLANES
    pitch = _seg_pitch(seg)
    n_hist = CONV_W - 1

    @pl.when(i == 0)
    def _():
        hist_scr[...] = hist_ref[...]
        hc_scr[...] = h0_ref[...]

    lam = lam_ref[...]
    half_coef = (-0.5 * LRU_C * LOG2E) * (jnp.maximum(-lam, 0.0)
                                         + jnp.log1p(jnp.exp(-jnp.abs(lam))))
    srow = lax.broadcasted_iota(jnp.int32, (SUBLANES, LRU_BLOCK), 0)
    new_hist = xr_ref[tt - n_hist:tt, :].astype(_F32)
    conv_ref[...] = new_hist

    def step_regs(v):
        return [v[g * SUBLANES:(g + 1) * SUBLANES] for g in range(seg)]

    for n in range(N_LRU_BLOCKS):
        cols = slice(n * LRU_BLOCK, (n + 1) * LRU_BLOCK)
        for s in range(SUBLANES):
            stage[n, s * pitch:s * pitch + seg, :] = xr_ref[s * seg:(s + 1) * seg, cols].astype(_F32)
        xg = [stage[n, pl.ds(g, SUBLANES, stride=pitch), :] for g in range(seg)]
        xt = jnp.concatenate(xg, axis=0)

        pre = []
        for j in range(1, n_hist + 1):
            hrow = jnp.broadcast_to(hist_scr[n_hist - j:n_hist - j + 1, cols], (SUBLANES, LRU_BLOCK))
            pre.append(jnp.where(srow == 0, hrow, pltpu.roll(xg[seg - j], 1, 0)))

        def back(j):
            head = [pre[k - 1] for k in range(j, 0, -1)]
            return jnp.concatenate(head + [xt[:tt - j * SUBLANES]], axis=0)

        xc = cb_ref[:, cols] + back(n_hist) * cw_ref[0:1, cols]
        for j in range(1, n_hist):
            xc = xc + back(n_hist - j) * cw_ref[j:j + 1, cols]
        xc = xc + xt * cw_ref[n_hist:CONV_W, cols]

        gates = jnp.dot(xc.astype(_BF16), wg_ref[n], preferred_element_type=_F32) + bg_ref[n]
        hc = half_coef[:, cols]
        a = jnp.exp2(hc * jnp.tanh(gates[:, :LRU_BLOCK]) + hc)
        mult = jnp.exp2(0.5 * jnp.log2(1.0 - a * a))
        half_xc = 0.5 * xc
        ix = half_xc * jnp.tanh(gates[:, LRU_BLOCK:]) + half_xc
        av = step_regs(a)
        bv = step_regs(mult * ix)
        if reset_first:
            first = jnp.logical_and(srow == 0, i == 0)
            av[0] = jnp.where(first, 0.0, av[0])
            bv[0] = jnp.where(first, ix[:SUBLANES], bv[0])

        seg_a, seg_b = av[0], bv[0]
        for g in range(1, seg):
            seg_b = av[g] * seg_b + bv[g]
            seg_a = av[g] * seg_a
        for d in (1, 2, 4):
            keep = srow >= d
            a_sh = jnp.where(keep, pltpu.roll(seg_a, d, 0), 1.0)
            b_sh = jnp.where(keep, pltpu.roll(seg_b, d, 0), 0.0)
            seg_b = seg_a * b_sh + seg_b
            seg_a = seg_a * a_sh
        h_in = jnp.broadcast_to(hc_scr[:, cols], (SUBLANES, LRU_BLOCK))
        h_end = seg_a * h_in + seg_b
        h = jnp.where(srow == 0, h_in, pltpu.roll(h_end, 1, 0))
        for g in range(seg):
            h = av[g] * h + bv[g]
            stage[n, pl.ds(g, SUBLANES, stride=pitch), :] = h
        hc_scr[:, cols] = h[SUBLANES - 1:SUBLANES]
        hlast_ref[:, cols] = h[SUBLANES - 1:SUBLANES]

        for s in range(SUBLANES):
            rows = slice(s * seg, (s + 1) * seg)
            gl = gl_ref[rows, cols].astype(_F32)
            hh = stage[n, s * pitch:s * pitch + seg, :] * (0.5 * gl)
            t = jnp.tanh(gl * (GELU_C1 + GELU_C2 * (gl * gl)))
            ya_ref[rows, cols] = (hh + hh * t).astype(_BF16)

    hist_scr[...] = new_hist


def _lru(z, hist, h0, conv_w, conv_b, wg, bg, lam, bsz, t, tt, reset_first):
    nt = t // tt
    kern = functools.partial(_lru_kernel, tt=tt, reset_first=reset_first)
    return pl.pallas_call(
        kern,
        grid=(bsz, nt),
        in_specs=[
            pl.BlockSpec((tt, D), lambda b, i: (b * nt + i, COL_XR)),
            pl.BlockSpec((tt, D), lambda b, i: (b * nt + i, COL_GL)),
            pl.BlockSpec((None, CONV_W - 1, D), lambda b, i: (b, 0, 0)),
            pl.BlockSpec((None, 1, D), lambda b, i: (b, 0, 0)),
            _resident((CONV_W, D)),
            _resident((1, D)),
            _resident((N_LRU_BLOCKS, LRU_BLOCK, 2 * LRU_BLOCK)),
            _resident((N_LRU_BLOCKS, 1, 2 * LRU_BLOCK)),
            _resident((1, D)),
        ],
        out_specs=[
            pl.BlockSpec((tt, D), lambda b, i: (b * nt + i, 0)),
            pl.BlockSpec((None, CONV_W - 1, D), lambda b, i: (b, 0, 0)),
            pl.BlockSpec((None, 1, D), lambda b, i: (b, 0, 0)),
        ],
        out_shape=[
            jax.ShapeDtypeStruct((bsz * t, D), _BF16),
            jax.ShapeDtypeStruct((bsz, CONV_W - 1, D), _F32),
            jax.ShapeDtypeStruct((bsz, 1, D), _F32),
        ],
        scratch_shapes=[
            pltpu.VMEM((N_LRU_BLOCKS, SUBLANES * _seg_pitch(tt // SUBLANES), LRU_BLOCK), _F32),
            pltpu.VMEM((CONV_W - 1, D), _F32),
            pltpu.VMEM((1, D), _F32),
        ],
        compiler_params=pltpu.CompilerParams(
            dimension_semantics=("arbitrary", "arbitrary"), vmem_limit_bytes=VMEM_LIMIT),
        name="lru",
    )(z, z, hist, h0, conv_w, conv_b, wg, bg, lam)


def _attn_kernel(q_ref, kp_ref, kc_ref, vp_ref, vc_ref, row0_ref, yb_ref, bias_scr, sc_scr,
                 *, tq, sb, band):
    i = pl.program_id(1)
    past = kp_ref.shape[0]
    win = past + sb
    nsub = tq // sb
    general = nsub if band else 0

    @pl.when(jnp.logical_and(pl.program_id(0) == 0, i == 0))
    def _build_bias():
        qq = lax.broadcasted_iota(jnp.int32, (sb, win), 0)
        kk = lax.broadcasted_iota(jnp.int32, (sb, win), 1)
        if band:
            dc = (kk // CHUNK) - (qq // CHUNK)
            ok = jnp.logical_and(dc >= 0, dc <= BAND_ROWS // CHUNK)
        else:
            ok = kk < past + CHUNK
        for h in range(N_HEADS):
            base = jnp.broadcast_to(row0_ref[h:h + 1, :], (sb, ROLL_W))
            toep = pltpu.roll(base, 0, 1, stride=1, stride_axis=0)[:, :win] * LOG2E
            gen = jnp.where(ok, toep, NEG_INF)
            bias_scr[general, h] = gen
            if band:
                for s in range(nsub):
                    bias_scr[s, h] = jnp.where(kk + s * sb >= past, gen, NEG_INF)

    def rows(prev_ref, cur_ref, so, cols):
        parts = []
        if so < past:
            parts.append(prev_ref[so:past, cols])
        parts.append(cur_ref[max(so - past, 0):so + sb, cols])
        return parts[0] if len(parts) == 1 else jnp.concatenate(parts, axis=0)

    def scores(n, s, h):
        cols = slice(h * HEAD_DIM, (h + 1) * HEAD_DIM)
        qh = q_ref[s * sb:(s + 1) * sb, cols]
        kh = rows(kp_ref, kc_ref, s * sb, cols)
        sc = lax.dot_general(qh, kh, (((1,), (1,)), ((), ())), preferred_element_type=_F32)
        sc_scr[n % SC_SLOTS] = sc + bias_scr[jnp.where(i == 0, s, general) if band else 0, h]

    def finish(n, s, h):
        cols = slice(h * HEAD_DIM, (h + 1) * HEAD_DIM)
        sc = sc_scr[n % SC_SLOTS]
        p = jnp.exp2(sc - jnp.max(sc, axis=-1, keepdims=True))
        l = jnp.sum(p, axis=-1, keepdims=True)
        vh = rows(vp_ref, vc_ref, s * sb, cols)
        o = jnp.dot(p.astype(_BF16), vh, preferred_element_type=_F32)
        yb_ref[s * sb:(s + 1) * sb, cols] = (o * (1.0 / l)).astype(_BF16)

    pairs = [(s, h) for s in range(nsub) for h in range(N_HEADS)]
    for n in range(min(SC_AHEAD, len(pairs))):
        scores(n, *pairs[n])
    for n, (s, h) in enumerate(pairs):
        if n + SC_AHEAD < len(pairs):
            scores(n + SC_AHEAD, *pairs[n + SC_AHEAD])
        finish(n, s, h)


def _attn(q_arr, q_col, kp_arr, kp_map, kc_arr, kc_col, vp_arr, vp_map, vc_arr, vc_col, row0,
          bsz, t, tq, sb, past, band):
    nt = t // tq
    win = past + sb
    n_bias = tq // sb + 1 if band else 1
    kern = functools.partial(_attn_kernel, tq=tq, sb=sb, band=band)
    return pl.pallas_call(
        kern,
        grid=(bsz, nt),
        in_specs=[
            pl.BlockSpec((tq, D), lambda b, i: (b * nt + i, q_col)),
            pl.BlockSpec((past, D), kp_map),
            pl.BlockSpec((tq, D), lambda b, i: (b * nt + i, kc_col)),
            pl.BlockSpec((past, D), vp_map),
            pl.BlockSpec((tq, D), lambda b, i: (b * nt + i, vc_col)),
            _resident((N_HEADS, ROLL_W)),
        ],
        out_specs=pl.BlockSpec((tq, D), lambda b, i: (b * nt + i, 0)),
        out_shape=jax.ShapeDtypeStruct((bsz * t, D), _BF16),
        scratch_shapes=[
            pltpu.VMEM((n_bias, N_HEADS, sb, win), _F32),
            pltpu.VMEM((SC_SLOTS, sb, win), _F32),
        ],
        compiler_params=pltpu.CompilerParams(
            dimension_semantics=("arbitrary", "arbitrary"), vmem_limit_bytes=VMEM_LIMIT),
        name="attn",
    )(q_arr, kp_arr, kc_arr, vp_arr, vc_arr, row0)


def _final_kernel(x_ref, ya_ref, yb_ref, ga_ref, gb_ref, mod_ref, wo_ref, wu_ref, bu_ref,
                  wd_ref, bd_ref, ln_ref, y_ref):
    g1 = mod_ref[:, 2 * D:3 * D]
    sh2 = mod_ref[:, 3 * D:4 * D]
    sc2 = mod_ref[:, 4 * D:5 * D]
    g2 = mod_ref[:, 5 * D:6 * D]
    merged = (_sigmoid(ga_ref[...].astype(_F32)) * ya_ref[...].astype(_F32)
              + _sigmoid(gb_ref[...].astype(_F32)) * yb_ref[...].astype(_F32))
    o = jnp.dot(merged.astype(_BF16), wo_ref[...], preferred_element_type=_F32)
    x1 = _layer_norm(ALPHA * x_ref[...] + (1.0 + g1) * o, ln_ref[0:1, :], ln_ref[1:2, :])
    u2 = (x1 * (1.0 + sc2) + sh2).astype(_BF16)
    f = bd_ref[...]
    for c in range(D_FF // D):
        cols = slice(c * D, (c + 1) * D)
        hid = jnp.dot(u2, wu_ref[:, cols], preferred_element_type=_F32) + bu_ref[:, cols]
        hid = jnp.square(jnp.maximum(hid, 0.0)).astype(_BF16)
        f = f + jnp.dot(hid, wd_ref[cols, :], preferred_element_type=_F32)
    y_ref[...] = _layer_norm(ALPHA * x1 + (1.0 + g2) * f, ln_ref[2:3, :], ln_ref[3:4, :])


def _final(x2, ya, yb, z, mod3, w_out, w_up, b_up, w_down, b_down, ln, tm, tiles_per_batch):
    m = x2.shape[0]
    row = lambda i: (i, 0)
    return pl.pallas_call(
        _final_kernel,
        grid=(m // tm,),
        in_specs=[
            pl.BlockSpec((tm, D), row),
            pl.BlockSpec((tm, D), row),
            pl.BlockSpec((tm, D), row),
            pl.BlockSpec((tm, D), lambda i: (i, COL_GA)),
            pl.BlockSpec((tm, D), lambda i: (i, COL_GB)),
            pl.BlockSpec((None, 1, 6 * D), lambda i: (i // tiles_per_batch, 0, 0)),
            _resident((D, D)),
            _resident((D, D_FF)),
            _resident((1, D_FF)),
            _resident((D_FF, D)),
            _resident((1, D)),
            _resident((4, D)),
        ],
        out_specs=pl.BlockSpec((tm, D), row),
        out_shape=jax.ShapeDtypeStruct((m, D), _F32),
        compiler_params=pltpu.CompilerParams(
            dimension_semantics=("arbitrary",), vmem_limit_bytes=VMEM_LIMIT),
        name="final",
    )(x2, ya, yb, z, z, mod3, w_out, w_up, b_up, w_down, b_down, ln)


def _rel_bias_row0(table, past):
    far = jnp.broadcast_to(table[:, 2 * MAX_REL:], (N_HEADS, past - MAX_REL + 1))
    near = table[:, 2 * MAX_REL - 1::-1]
    tail = jnp.broadcast_to(table[:, 2 * MAX_REL:],
                            (N_HEADS, ROLL_W - (past + MAX_REL + 1)))
    return jnp.concatenate([far, near, tail], axis=1).astype(_F32)


def _heads(a, bsz, rows):
    return a.reshape(1, bsz, rows, N_HEADS, HEAD_DIM)


@jax.jit
def _forward(x_prompt, x_sample, c_prompt, c_sample, cache_k, cache_v, state_conv, state_lru,
             w_ada, b_ada, w_in, conv_w, conv_b, w_rg, b_rg, w_ig, b_ig, lru_lambda, rel_bias,
             w_out, ln1_g, ln1_b, w_up, b_up, w_down, b_down, ln2_g, ln2_b):
    bp, t, _ = x_prompt.shape
    bs, s, _ = x_sample.shape
    l = 0
    w_in_b = w_in[l].astype(_BF16)
    w_out_b = w_out[l].astype(_BF16)
    w_up_b = w_up[l].astype(_BF16)
    w_down_b = w_down[l].astype(_BF16)
    wg = (0.5 * jnp.concatenate([w_rg[l], w_ig[l]], axis=-1)).astype(_BF16)
    bg = 0.5 * jnp.concatenate([b_rg[l], b_ig[l]], axis=-1)[:, None, :]
    lam = lru_lambda[l][None, :]
    cb = conv_b[l][None, :]
    ln = jnp.stack([ln1_g[l], ln1_b[l], ln2_g[l], ln2_b[l]])
    bu = b_up[l][None, :]
    bd = b_down[l][None, :]

    mod = _ada(jnp.concatenate([c_prompt, c_sample], axis=0), w_ada[l].astype(_BF16),
               b_ada[l][None, :])
    mod_p = mod[:bp, None, :]
    mod_s = mod[bp:, None, :]

    tm = 512
    xp2 = x_prompt.reshape(bp * t, D)
    z = _inproj(xp2, mod_p, w_in_b, tm, t // tm)
    ya, conv_p, h_p = _lru(z, jnp.zeros((bp, CONV_W - 1, D), _F32), jnp.zeros((bp, 1, D), _F32),
                           conv_w[l], cb, wg, bg, lam, bp, t, 512, True)
    tq, sb = 512, 128
    nt = t // tq
    row0 = _rel_bias_row0(rel_bias[l], BAND_ROWS)
    prev_k = lambda b, i: (b * nt + jnp.maximum(i - 1, 0), COL_K)
    prev_v = lambda b, i: (b * nt + jnp.maximum(i - 1, 0), COL_V)
    yb = _attn(z, COL_Q, z, prev_k, z, COL_K, z, prev_v, z, COL_V, row0,
               bp, t, tq, sb, BAND_ROWS, True)
    y_p = _final(xp2, ya, yb, z, mod_p, w_out_b, w_up_b, bu, w_down_b, bd, ln, tm, t // tm)
    z3 = z.reshape(bp, t, D_IN)
    rows = min(BAND_ROWS, t)
    k_p = z3[:, t - rows:, COL_K * D:(COL_K + 1) * D].astype(_F32)
    v_p = z3[:, t - rows:, COL_V * D:(COL_V + 1) * D].astype(_F32)

    xs2 = x_sample.reshape(bs * s, D)
    zs = _inproj(xs2, mod_s, w_in_b, s, 1)
    ya_s, conv_s, h_s = _lru(zs, state_conv[l], state_lru[l][:, None, :], conv_w[l], cb, wg, bg,
                             lam, bs, s, s, False)
    sbs = 2 * CHUNK
    zs3 = zs.reshape(bs, s, D_IN)
    qkv_s = jnp.pad(zs3[:, :, COL_Q * D:(COL_V + 1) * D], ((0, 0), (0, sbs - s), (0, 0)))
    qkv_s = qkv_s.reshape(bs * sbs, 3 * D)
    n_past = cache_k.shape[2]
    ck = cache_k[l].reshape(bs * n_past, D).astype(_BF16)
    cv = cache_v[l].reshape(bs * n_past, D).astype(_BF16)
    cache_map = lambda b, i: (b, 0)
    yb_s = _attn(qkv_s, 0, ck, cache_map, qkv_s, 1, cv, cache_map, qkv_s, 2,
                 _rel_bias_row0(rel_bias[l], n_past), bs, sbs, sbs, sbs, n_past, False)
    yb_s = yb_s.reshape(bs, sbs, D)[:, :s].reshape(bs * s, D)
    y_s = _final(xs2, ya_s, yb_s, zs, mod_s, w_out_b, w_up_b, bu, w_down_b, bd, ln, s, 1)
    k_s = zs3[:, :, COL_K * D:(COL_K + 1) * D].astype(_F32)
    v_s = zs3[:, :, COL_V * D:(COL_V + 1) * D].astype(_F32)

    return (y_p.reshape(bp, t, D), y_s.reshape(bs, s, D),
            _heads(k_p, bp, rows), _heads(v_p, bp, rows), conv_p[None], h_p.reshape(1, bp, D),
            _heads(k_s, bs, s), _heads(v_s, bs, s), conv_s[None], h_s.reshape(1, bs, D))


def kernel(x_prompt, x_sample, c_prompt, c_sample, cache_k, cache_v, state_conv, state_lru, w_ada, b_ada, w_in, conv_w, conv_b, w_rg, b_rg, w_ig, b_ig, lru_lambda, rel_bias, w_out, ln1_g, ln1_b, w_up, b_up, w_down, b_down, ln2_g, ln2_b):
    return _forward(x_prompt, x_sample, c_prompt, c_sample, cache_k, cache_v, state_conv,
                    state_lru, w_ada, b_ada, w_in, conv_w, conv_b, w_rg, b_rg, w_ig, b_ig,
                    lru_lambda, rel_bias, w_out, ln1_g, ln1_b, w_up, b_up, w_down, b_down,
                    ln2_g, ln2_b)
```

```python
import functools
import math

import jax
import jax.numpy as jnp
from jax import lax
from jax.experimental import pallas as pl
from jax.experimental.pallas import tpu as pltpu

D = 1024
CHUNK = 64
BAND_ROWS = 8 * CHUNK
N_HEADS = 8
HEAD_DIM = D // N_HEADS
N_LRU_BLOCKS = 8
LRU_BLOCK = D // N_LRU_BLOCKS
CONV_W = 4
LRU_C = 8.0
MAX_REL = 128
ATT_SCALE = HEAD_DIM ** -0.5
NEG_INF = -1e30
D_FF = 4 * D
D_IN = 7 * D
DEPTH = 1
ALPHA = (2 * DEPTH) ** 0.25
LN_EPS = 1e-5

LOG2E = math.log2(math.e)
Q_SCALE = ATT_SCALE * LOG2E
ROLL_W = 768
SC_AHEAD = 2
SC_SLOTS = SC_AHEAD + 1

GELU_C1 = math.sqrt(2.0 / math.pi)
GELU_C2 = GELU_C1 * 0.044715
SQRT_FLOOR = 1e-30

SUBLANES = 8
VMEM_LIMIT = 56 * 1024 * 1024

COL_XR, COL_GL, COL_Q, COL_K, COL_V, COL_GA, COL_GB = range(7)
Z_SOURCE_COLS = (COL_Q, COL_K, COL_V, COL_GA, COL_GB)
Z_Q, Z_K, Z_V, Z_GA, Z_GB = range(5)
D_Z = len(Z_SOURCE_COLS) * D
PROJ_W = 512
FINAL_ROW_GROUPS = 2
LRU_STAGES = 4

_F32 = jnp.float32
_BF16 = jnp.bfloat16


def _resident(shape):
    n = len(shape)
    return pl.BlockSpec(shape, lambda *_: (0,) * n, pipeline_mode=pl.Buffered(1))


def _sigmoid(v):
    return 0.5 * jnp.tanh(0.5 * v) + 0.5


def _layer_norm(v, g, b):
    mu = jnp.mean(v, axis=-1, keepdims=True)
    vc = v - mu
    var = jnp.mean(vc * vc, axis=-1, keepdims=True)
    return vc * lax.rsqrt(var + LN_EPS) * g + b


def _ada_kernel(c_ref, w_ref, b_ref, o_ref):
    c = c_ref[...]
    s = (c * _sigmoid(c)).astype(_BF16)
    o_ref[...] = jnp.dot(s, w_ref[...], preferred_element_type=_F32) + b_ref[...]


def _ada(c, w_ada, b_ada):
    n = c.shape[0]
    return pl.pallas_call(
        _ada_kernel,
        out_shape=jax.ShapeDtypeStruct((n, 6 * D), _F32),
        compiler_params=pltpu.CompilerParams(vmem_limit_bytes=VMEM_LIMIT),
        name="ada",
    )(c, w_ada, b_ada)


def _seg_pitch(seg):
    return seg if (seg // SUBLANES) % 2 == 1 else seg + SUBLANES


def _mix_in_kernel(x_ref, mod_ref, w_ref, hist_ref, h0_ref, cw_ref, cb_ref, wg_ref, bg_ref,
                   lam_ref, z_ref, ya_ref, conv_ref, hlast_ref, stage, gl_scr, hist_scr, hc_scr,
                   *, tt, reset_first):
    i = pl.program_id(1)
    seg = tt // SUBLANES
    pitch = _seg_pitch(seg)
    n_hist = CONV_W - 1

    @pl.when(i == 0)
    def _():
        hist_scr[...] = hist_ref[...]
        hc_scr[...] = h0_ref[...]

    lam = lam_ref[...]
    half_coef = (-0.5 * LRU_C * LOG2E) * (jnp.maximum(-lam, 0.0)
                                         + jnp.log1p(jnp.exp(-jnp.abs(lam))))
    srow = lax.broadcasted_iota(jnp.int32, (SUBLANES, LRU_BLOCK), 0)

    sh1 = mod_ref[:, 0:D]
    sc1 = mod_ref[:, D:2 * D]
    u = (x_ref[...] * (1.0 + sc1) + sh1).astype(_BF16)

    def project(c0):
        return jnp.dot(u, w_ref[:, c0:c0 + PROJ_W], preferred_element_type=_F32)

    def project_rnn(k):
        c0 = k * PROJ_W
        xr = project(COL_XR * D + c0)
        conv_ref[:, c0:c0 + PROJ_W] = xr[tt - n_hist:tt, :]
        for m in range(PROJ_W // LRU_BLOCK):
            for s in range(SUBLANES):
                stage[k * (PROJ_W // LRU_BLOCK) + m, s * pitch:s * pitch + seg, :] = (
                    xr[s * seg:(s + 1) * seg, m * LRU_BLOCK:(m + 1) * LRU_BLOCK])
        gl_scr[:, c0:c0 + PROJ_W] = project(COL_GL * D + c0)

    def project_z(k):
        col, off = divmod(k * PROJ_W, D)
        zk = project(Z_SOURCE_COLS[col] * D + off)
        if Z_SOURCE_COLS[col] == COL_Q:
            zk = zk * Q_SCALE
        z_ref[:, k * PROJ_W:(k + 1) * PROJ_W] = zk.astype(_BF16)

    def step_regs(v):
        return [v[g * SUBLANES:(g + 1) * SUBLANES] for g in range(seg)]

    def lru_block(n):
        cols = slice(n * LRU_BLOCK, (n + 1) * LRU_BLOCK)
        xg = [stage[n, pl.ds(g, SUBLANES, stride=pitch), :] for g in range(seg)]
        xt = jnp.concatenate(xg, axis=0)

        pre = []
        for j in range(1, n_hist + 1):
            hrow = jnp.broadcast_to(hist_scr[n_hist - j:n_hist - j + 1, cols], (SUBLANES, LRU_BLOCK))
            pre.append(jnp.where(srow == 0, hrow, pltpu.roll(xg[seg - j], 1, 0)))

        def back(j):
            head = [pre[k - 1] for k in range(j, 0, -1)]
            return jnp.concatenate(head + [xt[:tt - j * SUBLANES]], axis=0)

        xc = cb_ref[:, cols] + back(n_hist) * cw_ref[0:1, cols]
        for j in range(1, n_hist):
            xc = xc + back(n_hist - j) * cw_ref[j:j + 1, cols]
        xc = xc + xt * cw_ref[n_hist:CONV_W, cols]

        yield
        gates = jnp.dot(xc.astype(_BF16), wg_ref[n], preferred_element_type=_F32) + bg_ref[n]
        hc = half_coef[:, cols]
        a = jnp.exp2(hc * jnp.tanh(gates[:, :LRU_BLOCK]) + hc)
        mult = jnp.exp2(0.5 * jnp.log2(1.0 - a * a))
        half_xc = 0.5 * xc
        ix = half_xc * jnp.tanh(gates[:, LRU_BLOCK:]) + half_xc
        av = step_regs(a)
        bv = step_regs(mult * ix)
        if reset_first:
            first = jnp.logical_and(srow == 0, i == 0)
            av[0] = jnp.where(first, 0.0, av[0])
            bv[0] = jnp.where(first, ix[:SUBLANES], bv[0])

        yield
        seg_a, seg_b = av[0], bv[0]
        for g in range(1, seg):
            seg_b = av[g] * seg_b + bv[g]
            seg_a = av[g] * seg_a
        for d in (1, 2, 4):
            keep = srow >= d
            a_sh = jnp.where(keep, pltpu.roll(seg_a, d, 0), 1.0)
            b_sh = jnp.where(keep, pltpu.roll(seg_b, d, 0), 0.0)
            seg_b = seg_a * b_sh + seg_b
            seg_a = seg_a * a_sh
        yield
        h_in = jnp.broadcast_to(hc_scr[:, cols], (SUBLANES, LRU_BLOCK))
        h_end = seg_a * h_in + seg_b
        h = jnp.where(srow == 0, h_in, pltpu.roll(h_end, 1, 0))
        for g in range(seg):
            h = av[g] * h + bv[g]
            stage[n, pl.ds(g, SUBLANES, stride=pitch), :] = h
        hc_scr[:, cols] = h[SUBLANES - 1:SUBLANES]
        hlast_ref[:, cols] = h[SUBLANES - 1:SUBLANES]

        yield
        for s in range(SUBLANES):
            rows = slice(s * seg, (s + 1) * seg)
            gl = gl_scr[rows, cols]
            hh = stage[n, s * pitch:s * pitch + seg, :] * (0.5 * gl)
            t = jnp.tanh(gl * (GELU_C1 + GELU_C2 * (gl * gl)))
            ya_ref[rows, cols] = (hh + hh * t).astype(_BF16)

    blocks_per_chunk = PROJ_W // LRU_BLOCK
    n_z = D_Z // PROJ_W
    n_slots = N_LRU_BLOCKS * LRU_STAGES
    slot = 0
    project_rnn(0)
    for n in range(N_LRU_BLOCKS):
        if n % blocks_per_chunk == 0 and n + blocks_per_chunk < N_LRU_BLOCKS:
            project_rnn(n // blocks_per_chunk + 1)
        for _ in lru_block(n):
            for k in range(slot * n_z // n_slots, (slot + 1) * n_z // n_slots):
                project_z(k)
            slot += 1
    assert slot == n_slots

    hist_scr[...] = conv_ref[...]


def _mix_in(x2, mod3, w_in, hist, h0, conv_w, conv_b, wg, bg, lam, bsz, t, tt, reset_first):
    nt = t // tt
    kern = functools.partial(_mix_in_kernel, tt=tt, reset_first=reset_first)
    return pl.pallas_call(
        kern,
        grid=(bsz, nt),
        in_specs=[
            pl.BlockSpec((tt, D), lambda b, i: (b * nt + i, 0)),
            pl.BlockSpec((None, 1, 6 * D), lambda b, i: (b, 0, 0)),
            _resident((D, D_IN)),
            pl.BlockSpec((None, CONV_W - 1, D), lambda b, i: (b, 0, 0)),
            pl.BlockSpec((None, 1, D), lambda b, i: (b, 0, 0)),
            _resident((CONV_W, D)),
            _resident((1, D)),
            _resident((N_LRU_BLOCKS, LRU_BLOCK, 2 * LRU_BLOCK)),
            _resident((N_LRU_BLOCKS, 1, 2 * LRU_BLOCK)),
            _resident((1, D)),
        ],
        out_specs=[
            pl.BlockSpec((tt, D_Z), lambda b, i: (b * nt + i, 0)),
            pl.BlockSpec((tt, D), lambda b, i: (b * nt + i, 0)),
            pl.BlockSpec((None, CONV_W - 1, D), lambda b, i: (b, 0, 0)),
            pl.BlockSpec((None, 1, D), lambda b, i: (b, 0, 0)),
        ],
        out_shape=[
            jax.ShapeDtypeStruct((bsz * t, D_Z), _BF16),
            jax.ShapeDtypeStruct((bsz * t, D), _BF16),
            jax.ShapeDtypeStruct((bsz, CONV_W - 1, D), _F32),
            jax.ShapeDtypeStruct((bsz, 1, D), _F32),
        ],
        scratch_shapes=[
            pltpu.VMEM((N_LRU_BLOCKS, SUBLANES * _seg_pitch(tt // SUBLANES), LRU_BLOCK), _F32),
            pltpu.VMEM((tt, D), _F32),
            pltpu.VMEM((CONV_W - 1, D), _F32),
            pltpu.VMEM((1, D), _F32),
        ],
        compiler_params=pltpu.CompilerParams(
            dimension_semantics=("arbitrary", "arbitrary"), vmem_limit_bytes=VMEM_LIMIT),
        name="mix_in",
    )(x2, mod3, w_in, hist, h0, conv_w, conv_b, wg, bg, lam)


def _attn_kernel(q_ref, kp_ref, kc_ref, vp_ref, vc_ref, row0_ref, yb_ref, bias_scr, sc_scr,
                 *, tq, sb, band):
    i = pl.program_id(1)
    past = kp_ref.shape[0]
    win = past + sb
    nsub = tq // sb
    general = nsub if band else 0

    @pl.when(jnp.logical_and(pl.program_id(0) == 0, i == 0))
    def _build_bias():
        qq = lax.broadcasted_iota(jnp.int32, (sb, win), 0)
        kk = lax.broadcasted_iota(jnp.int32, (sb, win), 1)
        if band:
            dc = (kk // CHUNK) - (qq // CHUNK)
            ok = jnp.logical_and(dc >= 0, dc <= BAND_ROWS // CHUNK)
        else:
            ok = kk < past + CHUNK
        for h in range(N_HEADS):
            base = jnp.broadcast_to(row0_ref[h:h + 1, :], (sb, ROLL_W))
            toep = pltpu.roll(base, 0, 1, stride=1, stride_axis=0)[:, :win] * LOG2E
            gen = jnp.where(ok, toep, NEG_INF)
            bias_scr[general, h] = gen
            if band:
                for s in range(nsub):
                    bias_scr[s, h] = jnp.where(kk + s * sb >= past, gen, NEG_INF)

    def rows(prev_ref, cur_ref, so, cols):
        parts = []
        if so < past:
            parts.append(prev_ref[so:past, cols])
        parts.append(cur_ref[max(so - past, 0):so + sb, cols])
        return parts[0] if len(parts) == 1 else jnp.concatenate(parts, axis=0)

    def scores(n, s, h):
        cols = slice(h * HEAD_DIM, (h + 1) * HEAD_DIM)
        qh = q_ref[s * sb:(s + 1) * sb, cols]
        kh = rows(kp_ref, kc_ref, s * sb, cols)
        sc = lax.dot_general(qh, kh, (((1,), (1,)), ((), ())), preferred_element_type=_F32)
        sc_scr[n % SC_SLOTS] = sc + bias_scr[jnp.where(i == 0, s, general) if band else 0, h]

    def finish(n, s, h):
        cols = slice(h * HEAD_DIM, (h + 1) * HEAD_DIM)
        sc = sc_scr[n % SC_SLOTS]
        p = jnp.exp2(sc - jnp.max(sc, axis=-1, keepdims=True))
        l = jnp.sum(p, axis=-1, keepdims=True)
        vh = rows(vp_ref, vc_ref, s * sb, cols)
        o = jnp.dot(p.astype(_BF16), vh, preferred_element_type=_F32)
        yb_ref[s * sb:(s + 1) * sb, cols] = (o * (1.0 / l)).astype(_BF16)

    pairs = [(s, h) for s in range(nsub) for h in range(N_HEADS)]
    for n in range(min(SC_AHEAD, len(pairs))):
        scores(n, *pairs[n])
    for n, (s, h) in enumerate(pairs):
        if n + SC_AHEAD < len(pairs):
            scores(n + SC_AHEAD, *pairs[n + SC_AHEAD])
        finish(n, s, h)


def _attn(q_arr, q_col, kp_arr, kp_map, kc_arr, kc_col, vp_arr, vp_map, vc_arr, vc_col, row0,
          bsz, t, tq, sb, past, band):
    nt = t // tq
    win = past + sb
    n_bias = tq // sb + 1 if band else 1
    kern = functools.partial(_attn_kernel, tq=tq, sb=sb, band=band)
    return pl.pallas_call(
        kern,
        grid=(bsz, nt),
        in_specs=[
            pl.BlockSpec((tq, D), lambda b, i: (b * nt + i, q_col)),
            pl.BlockSpec((past, D), kp_map),
            pl.BlockSpec((tq, D), lambda b, i: (b * nt + i, kc_col)),
            pl.BlockSpec((past, D), vp_map),
            pl.BlockSpec((tq, D), lambda b, i: (b * nt + i, vc_col)),
            _resident((N_HEADS, ROLL_W)),
        ],
        out_specs=pl.BlockSpec((tq, D), lambda b, i: (b * nt + i, 0)),
        out_shape=jax.ShapeDtypeStruct((bsz * t, D), _BF16),
        scratch_shapes=[
            pltpu.VMEM((n_bias, N_HEADS, sb, win), _F32),
            pltpu.VMEM((SC_SLOTS, sb, win), _F32),
        ],
        compiler_params=pltpu.CompilerParams(
            dimension_semantics=("arbitrary", "arbitrary"), vmem_limit_bytes=VMEM_LIMIT),
        name="attn",
    )(q_arr, kp_arr, kc_arr, vp_arr, vc_arr, row0)


def _final_kernel(x_ref, ya_ref, yb_ref, ga_ref, gb_ref, mod_ref, wo_ref, wu_ref, bu_ref,
                  wd_ref, bd_ref, ln_ref, y_ref):
    g1 = mod_ref[:, 2 * D:3 * D]
    sh2 = mod_ref[:, 3 * D:4 * D]
    sc2 = mod_ref[:, 4 * D:5 * D]
    g2 = mod_ref[:, 5 * D:6 * D]
    def row_group(rows):
        merged = (_sigmoid(ga_ref[rows, :].astype(_F32)) * ya_ref[rows, :].astype(_F32)
                  + _sigmoid(gb_ref[rows, :].astype(_F32)) * yb_ref[rows, :].astype(_F32))
        merged = merged.astype(_BF16)
        yield
        o = jnp.dot(merged, wo_ref[...], preferred_element_type=_F32)
        yield
        x1 = _layer_norm(ALPHA * x_ref[rows, :] + (1.0 + g1) * o, ln_ref[0:1, :], ln_ref[1:2, :])
        u2 = (x1 * (1.0 + sc2) + sh2).astype(_BF16)
        yield
        f = bd_ref[...]
        for c in range(D_FF // D):
            cols = slice(c * D, (c + 1) * D)
            hid = jnp.dot(u2, wu_ref[:, cols], preferred_element_type=_F32) + bu_ref[:, cols]
            hid = jnp.square(jnp.maximum(hid, 0.0)).astype(_BF16)
            f = f + jnp.dot(hid, wd_ref[cols, :], preferred_element_type=_F32)
            yield
        y_ref[rows, :] = _layer_norm(ALPHA * x1 + (1.0 + g2) * f, ln_ref[2:3, :], ln_ref[3:4, :])

    tm = x_ref.shape[0]
    n_groups = FINAL_ROW_GROUPS if tm % (FINAL_ROW_GROUPS * 16) == 0 else 1
    gm = tm // n_groups
    groups = [row_group(slice(k * gm, (k + 1) * gm)) for k in range(n_groups)]
    live = list(range(n_groups))
    tick = 0
    while live:
        for k in list(live):
            if tick >= k:
                if next(groups[k], StopIteration) is StopIteration:
                    live.remove(k)
        tick += 1


def _final(x2, ya, yb, z, mod3, w_out, w_up, b_up, w_down, b_down, ln, tm, tiles_per_batch):
    m = x2.shape[0]
    row = lambda i: (i, 0)
    return pl.pallas_call(
        _final_kernel,
        grid=(m // tm,),
        in_specs=[
            pl.BlockSpec((tm, D), row),
            pl.BlockSpec((tm, D), row),
            pl.BlockSpec((tm, D), row),
            pl.BlockSpec((tm, D), lambda i: (i, Z_GA)),
            pl.BlockSpec((tm, D), lambda i: (i, Z_GB)),
            pl.BlockSpec((None, 1, 6 * D), lambda i: (i // tiles_per_batch, 0, 0)),
            _resident((D, D)),
            _resident((D, D_FF)),
            _resident((1, D_FF)),
            _resident((D_FF, D)),
            _resident((1, D)),
            _resident((4, D)),
        ],
        out_specs=pl.BlockSpec((tm, D), row),
        out_shape=jax.ShapeDtypeStruct((m, D), _F32),
        compiler_params=pltpu.CompilerParams(
            dimension_semantics=("arbitrary",), vmem_limit_bytes=VMEM_LIMIT),
        name="final",
    )(x2, ya, yb, z, z, mod3, w_out, w_up, b_up, w_down, b_down, ln)


def _rel_bias_row0(table, past):
    far = jnp.broadcast_to(table[:, 2 * MAX_REL:], (N_HEADS, past - MAX_REL + 1))
    near = table[:, 2 * MAX_REL - 1::-1]
    tail = jnp.broadcast_to(table[:, 2 * MAX_REL:],
                            (N_HEADS, ROLL_W - (past + MAX_REL + 1)))
    return jnp.concatenate([far, near, tail], axis=1).astype(_F32)


def _heads(a, bsz, rows):
    return a.reshape(1, bsz, rows, N_HEADS, HEAD_DIM)


@jax.jit
def _forward(x_prompt, x_sample, c_prompt, c_sample, cache_k, cache_v, state_conv, state_lru,
             w_ada, b_ada, w_in, conv_w, conv_b, w_rg, b_rg, w_ig, b_ig, lru_lambda, rel_bias,
             w_out, ln1_g, ln1_b, w_up, b_up, w_down, b_down, ln2_g, ln2_b):
    bp, t, _ = x_prompt.shape
    bs, s, _ = x_sample.shape
    l = 0
    w_in_b = w_in[l].astype(_BF16)
    w_out_b = w_out[l].astype(_BF16)
    w_up_b = w_up[l].astype(_BF16)
    w_down_b = w_down[l].astype(_BF16)
    wg = (0.5 * jnp.concatenate([w_rg[l], w_ig[l]], axis=-1)).astype(_BF16)
    bg = 0.5 * jnp.concatenate([b_rg[l], b_ig[l]], axis=-1)[:, None, :]
    lam = lru_lambda[l][None, :]
    cb = conv_b[l][None, :]
    ln = jnp.stack([ln1_g[l], ln1_b[l], ln2_g[l], ln2_b[l]])
    bu = b_up[l][None, :]
    bd = b_down[l][None, :]

    mod = _ada(jnp.concatenate([c_prompt, c_sample], axis=0), w_ada[l].astype(_BF16),
               b_ada[l][None, :])
    mod_p = mod[:bp, None, :]
    mod_s = mod[bp:, None, :]

    tm = 512
    xp2 = x_prompt.reshape(bp * t, D)
    z, ya, conv_p, h_p = _mix_in(xp2, mod_p, w_in_b, jnp.zeros((bp, CONV_W - 1, D), _F32),
                                 jnp.zeros((bp, 1, D), _F32), conv_w[l], cb, wg, bg, lam,
                                 bp, t, tm, True)
    tq, sb = 512, 128
    nt = t // tq
    row0 = _rel_bias_row0(rel_bias[l], BAND_ROWS)
    prev_k = lambda b, i: (b * nt + jnp.maximum(i - 1, 0), Z_K)
    prev_v = lambda b, i: (b * nt + jnp.maximum(i - 1, 0), Z_V)
    yb = _attn(z, Z_Q, z, prev_k, z, Z_K, z, prev_v, z, Z_V, row0,
               bp, t, tq, sb, BAND_ROWS, True)
    y_p = _final(xp2, ya, yb, z, mod_p, w_out_b, w_up_b, bu, w_down_b, bd, ln, tm, t // tm)
    z3 = z.reshape(bp, t, D_Z)
    rows = min(BAND_ROWS, t)
    k_p = z3[:, t - rows:, Z_K * D:(Z_K + 1) * D].astype(_F32)
    v_p = z3[:, t - rows:, Z_V * D:(Z_V + 1) * D].astype(_F32)

    xs2 = x_sample.reshape(bs * s, D)
    zs, ya_s, conv_s, h_s = _mix_in(xs2, mod_s, w_in_b, state_conv[l], state_lru[l][:, None, :],
                                    conv_w[l], cb, wg, bg, lam, bs, s, s, False)
    sbs = 2 * CHUNK
    zs3 = zs.reshape(bs, s, D_Z)
    qkv_s = jnp.pad(zs3[:, :, Z_Q * D:(Z_V + 1) * D], ((0, 0), (0, sbs - s), (0, 0)))
    qkv_s = qkv_s.reshape(bs * sbs, 3 * D)
    n_past = cache_k.shape[2]
    ck = cache_k[l].reshape(bs * n_past, D).astype(_BF16)
    cv = cache_v[l].reshape(bs * n_past, D).astype(_BF16)
    cache_map = lambda b, i: (b, 0)
    yb_s = _attn(qkv_s, 0, ck, cache_map, qkv_s, 1, cv, cache_map, qkv_s, 2,
                 _rel_bias_row0(rel_bias[l], n_past), bs, sbs, sbs, sbs, n_past, False)
    yb_s = yb_s.reshape(bs, sbs, D)[:, :s].reshape(bs * s, D)
    y_s = _final(xs2, ya_s, yb_s, zs, mod_s, w_out_b, w_up_b, bu, w_down_b, bd, ln, s, 1)
    k_s = zs3[:, :, Z_K * D:(Z_K + 1) * D].astype(_F32)
    v_s = zs3[:, :, Z_V * D:(Z_V + 1) * D].astype(_F32)

    return (y_p.reshape(bp, t, D), y_s.reshape(bs, s, D),
            _heads(k_p, bp, rows), _heads(v_p, bp, rows), conv_p[None], h_p.reshape(1, bp, D),
            _heads(k_s, bs, s), _heads(v_s, bs, s), conv_s[None], h_s.reshape(1, bs, D))


def kernel(x_prompt, x_sample, c_prompt, c_sample, cache_k, cache_v, state_conv, state_lru, w_ada, b_ada, w_in, conv_w, conv_b, w_rg, b_rg, w_ig, b_ig, lru_lambda, rel_bias, w_out, ln1_g, ln1_b, w_up, b_up, w_down, b_down, ln2_g, ln2_b):
    return _forward(x_prompt, x_sample, c_prompt, c_sample, cache_k, cache_v, state_conv,
                    state_lru, w_ada, b_ada, w_in, conv_w, conv_b, w_rg, b_rg, w_ig, b_ig,
                    lru_lambda, rel_bias, w_out, ln1_g, ln1_b, w_up, b_up, w_down, b_down,
                    ln2_g, ln2_b)
```

```python
import functools
import math

import jax
import jax.numpy as jnp
from jax import lax
from jax.experimental import pallas as pl
from jax.experimental.pallas import tpu as pltpu

D = 1024
CHUNK = 64
BAND_ROWS = 8 * CHUNK
N_HEADS = 8
HEAD_DIM = D // N_HEADS
N_LRU_BLOCKS = 8
LRU_BLOCK = D // N_LRU_BLOCKS
CONV_W = 4
LRU_C = 8.0
MAX_REL = 128
ATT_SCALE = HEAD_DIM ** -0.5
NEG_INF = -1e30
D_FF = 4 * D
D_IN = 7 * D
DEPTH = 1
ALPHA = (2 * DEPTH) ** 0.25
LN_EPS = 1e-5

LOG2E = math.log2(math.e)
Q_SCALE = ATT_SCALE * LOG2E
ROLL_W = 768
SC_AHEAD = 2
SC_SLOTS = SC_AHEAD + 1
GELU_C1 = math.sqrt(2.0 / math.pi)
GELU_C2 = GELU_C1 * 0.044715
SQRT_FLOOR = 1e-30

SUBLANES = 8
VMEM_LIMIT = 56 * 1024 * 1024

COL_XR, COL_GL, COL_Q, COL_K, COL_V, COL_GA, COL_GB = range(7)
Z_SOURCE_COLS = (COL_Q, COL_K, COL_V, COL_GB)
Z_Q, Z_K, Z_V, Z_GB = range(4)
D_Z = len(Z_SOURCE_COLS) * D
PROJ_W = 256
FINAL_ROW_GROUPS = 2
FINAL_MIN_GROUP_ROWS = 256
Z_ROW_SPLIT = 2
Z_MIN_ROWS = 256
LRU_STAGES = 7

_F32 = jnp.float32
_BF16 = jnp.bfloat16


def _resident(shape):
    n = len(shape)
    return pl.BlockSpec(shape, lambda *_: (0,) * n, pipeline_mode=pl.Buffered(1))


def _sigmoid(v):
    return 0.5 * jnp.tanh(0.5 * v) + 0.5


def _layer_norm(v, g, b):
    mu = jnp.mean(v, axis=-1, keepdims=True)
    vc = v - mu
    var = jnp.mean(vc * vc, axis=-1, keepdims=True)
    return vc * lax.rsqrt(var + LN_EPS) * g + b


def _ada_kernel(c_ref, w_ref, b_ref, o_ref):
    c = c_ref[...]
    s = (c * _sigmoid(c)).astype(_BF16)
    o_ref[...] = jnp.dot(s, w_ref[...], preferred_element_type=_F32) + b_ref[...]


def _ada(c, w_ada, b_ada):
    n = c.shape[0]
    return pl.pallas_call(
        _ada_kernel,
        out_shape=jax.ShapeDtypeStruct((n, 6 * D), _F32),
        compiler_params=pltpu.CompilerParams(vmem_limit_bytes=VMEM_LIMIT),
        name="ada",
    )(c, w_ada, b_ada)


def _seg_pitch(seg):
    return seg if (seg // SUBLANES) % 2 == 1 else seg + SUBLANES


def _mix_in_kernel(x_ref, mod_ref, w_ref, hist_ref, h0_ref, cw_ref, cb_ref, wg_ref, bg_ref,
                   lam_ref, z_ref, ya_ref, conv_ref, hlast_ref, stage, gl_scr, ga_scr, hist_scr,
                   hc_scr, *, tt, reset_first):
    i = pl.program_id(1)
    seg = tt // SUBLANES
    pitch = _seg_pitch(seg)
    n_hist = CONV_W - 1

    @pl.when(i == 0)
    def _():
        hist_scr[...] = hist_ref[...]
        hc_scr[...] = h0_ref[...]

    lam = lam_ref[...]
    half_coef = (-0.5 * LRU_C * LOG2E) * (jnp.maximum(-lam, 0.0)
                                         + jnp.log1p(jnp.exp(-jnp.abs(lam))))
    srow = lax.broadcasted_iota(jnp.int32, (SUBLANES, LRU_BLOCK), 0)

    sh1 = mod_ref[:, 0:D]
    sc1 = mod_ref[:, D:2 * D]
    u = (x_ref[...] * (1.0 + sc1) + sh1).astype(_BF16)

    def project(c0):
        return jnp.dot(u, w_ref[:, c0:c0 + PROJ_W], preferred_element_type=_F32)

    def project_rnn(k):
        c0 = k * PROJ_W
        xr = project(COL_XR * D + c0)
        conv_ref[:, c0:c0 + PROJ_W] = xr[tt - n_hist:tt, :]
        for m in range(PROJ_W // LRU_BLOCK):
            for s in range(SUBLANES):
                stage[k * (PROJ_W // LRU_BLOCK) + m, s * pitch:s * pitch + seg, :] = (
                    xr[s * seg:(s + 1) * seg, m * LRU_BLOCK:(m + 1) * LRU_BLOCK])
        gl_scr[:, c0:c0 + PROJ_W] = project(COL_GL * D + c0)
        ga_scr[:, c0:c0 + PROJ_W] = project(COL_GA * D + c0)

    z_rows = tt // Z_ROW_SPLIT if tt >= Z_ROW_SPLIT * Z_MIN_ROWS else tt

    def project_z(k):
        kc, kr = divmod(k, tt // z_rows)
        col, off = divmod(kc * PROJ_W, D)
        c0 = Z_SOURCE_COLS[col] * D + off
        rows = slice(kr * z_rows, (kr + 1) * z_rows)
        zk = jnp.dot(u[rows], w_ref[:, c0:c0 + PROJ_W], preferred_element_type=_F32)
        if Z_SOURCE_COLS[col] == COL_Q:
            zk = zk * Q_SCALE
        elif Z_SOURCE_COLS[col] == COL_GB:
            zk = 0.5 * jnp.tanh(zk) + 0.5
        z_ref[rows, kc * PROJ_W:(kc + 1) * PROJ_W] = zk.astype(_BF16)

    def step_regs(v):
        return [v[g * SUBLANES:(g + 1) * SUBLANES] for g in range(seg)]

    def lru_block(n):
        cols = slice(n * LRU_BLOCK, (n + 1) * LRU_BLOCK)
        xg = [stage[n, pl.ds(g, SUBLANES, stride=pitch), :] for g in range(seg)]
        xt = jnp.concatenate(xg, axis=0)

        pre = []
        for j in range(1, n_hist + 1):
            hrow = jnp.broadcast_to(hist_scr[n_hist - j:n_hist - j + 1, cols], (SUBLANES, LRU_BLOCK))
            pre.append(jnp.where(srow == 0, hrow, pltpu.roll(xg[seg - j], 1, 0)))

        def back(j):
            head = [pre[k - 1] for k in range(j, 0, -1)]
            return jnp.concatenate(head + [xt[:tt - j * SUBLANES]], axis=0)

        xc = cb_ref[:, cols] + back(n_hist) * cw_ref[0:1, cols]
        for j in range(1, n_hist):
            xc = xc + back(n_hist - j) * cw_ref[j:j + 1, cols]
        xc = xc + xt * cw_ref[n_hist:CONV_W, cols]

        yield
        gates = jnp.dot(xc.astype(_BF16), wg_ref[n], preferred_element_type=_F32) + bg_ref[n]
        hc = half_coef[:, cols]
        a = jnp.exp2(hc * jnp.tanh(gates[:, :LRU_BLOCK]) + hc)
        mult = jnp.exp2(0.5 * jnp.log2(1.0 - a * a))
        half_xc = 0.5 * xc
        ix = half_xc * jnp.tanh(gates[:, LRU_BLOCK:]) + half_xc
        av = step_regs(a)
        bv = step_regs(mult * ix)
        if reset_first:
            first = jnp.logical_and(srow == 0, i == 0)
            av[0] = jnp.where(first, 0.0, av[0])
            bv[0] = jnp.where(first, ix[:SUBLANES], bv[0])

        yield
        seg_a, seg_b = av[0], bv[0]
        for g in range(1, seg):
            seg_b = av[g] * seg_b + bv[g]
            seg_a = av[g] * seg_a
            if g == seg // 2:
                yield
        for d in (1, 2, 4):
            keep = srow >= d
            a_sh = jnp.where(keep, pltpu.roll(seg_a, d, 0), 1.0)
            b_sh = jnp.where(keep, pltpu.roll(seg_b, d, 0), 0.0)
            seg_b = seg_a * b_sh + seg_b
            seg_a = seg_a * a_sh
        yield
        h_in = jnp.broadcast_to(hc_scr[:, cols], (SUBLANES, LRU_BLOCK))
        h_end = seg_a * h_in + seg_b
        h = jnp.where(srow == 0, h_in, pltpu.roll(h_end, 1, 0))
        for g in range(seg):
            h = av[g] * h + bv[g]
            stage[n, pl.ds(g, SUBLANES, stride=pitch), :] = h
            if g == seg // 2:
                yield
        hc_scr[:, cols] = h[SUBLANES - 1:SUBLANES]
        hlast_ref[:, cols] = h[SUBLANES - 1:SUBLANES]

        yield
        for s in range(SUBLANES):
            rows = slice(s * seg, (s + 1) * seg)
            gl = gl_scr[rows, cols]
            hq = stage[n, s * pitch:s * pitch + seg, :] * (0.25 * gl)
            w = hq + hq * jnp.tanh(gl * (GELU_C1 + GELU_C2 * (gl * gl)))
            ya_ref[rows, cols] = (w + w * jnp.tanh(ga_scr[rows, cols])).astype(_BF16)
            if s == SUBLANES // 2 - 1:
                yield

    blocks_per_chunk = PROJ_W // LRU_BLOCK
    n_z = (D_Z // PROJ_W) * (tt // z_rows)
    n_slots = N_LRU_BLOCKS * LRU_STAGES
    slot = 0
    project_rnn(0)
    for n in range(N_LRU_BLOCKS):
        if n % blocks_per_chunk == 0 and n + blocks_per_chunk < N_LRU_BLOCKS:
            project_rnn(n // blocks_per_chunk + 1)
        for _ in lru_block(n):
            for k in range(slot * n_z // n_slots, (slot + 1) * n_z // n_slots):
                project_z(k)
            slot += 1
    assert slot == n_slots

    hist_scr[...] = conv_ref[...]


def _mix_in(x2, mod3, w_in, hist, h0, conv_w, conv_b, wg, bg, lam, bsz, t, tt, reset_first):
    nt = t // tt
    kern = functools.partial(_mix_in_kernel, tt=tt, reset_first=reset_first)
    return pl.pallas_call(
        kern,
        grid=(bsz, nt),
        in_specs=[
            pl.BlockSpec((tt, D), lambda b, i: (b * nt + i, 0)),
            pl.BlockSpec((None, 1, 6 * D), lambda b, i: (b, 0, 0)),
            _resident((D, D_IN)),
            pl.BlockSpec((None, CONV_W - 1, D), lambda b, i: (b, 0, 0)),
            pl.BlockSpec((None, 1, D), lambda b, i: (b, 0, 0)),
            _resident((CONV_W, D)),
            _resident((1, D)),
            _resident((N_LRU_BLOCKS, LRU_BLOCK, 2 * LRU_BLOCK)),
            _resident((N_LRU_BLOCKS, 1, 2 * LRU_BLOCK)),
            _resident((1, D)),
        ],
        out_specs=[
            pl.BlockSpec((tt, D_Z), lambda b, i: (b * nt + i, 0)),
            pl.BlockSpec((tt, D), lambda b, i: (b * nt + i, 0)),
            pl.BlockSpec((None, CONV_W - 1, D), lambda b, i: (b, 0, 0)),
            pl.BlockSpec((None, 1, D), lambda b, i: (b, 0, 0)),
        ],
        out_shape=[
            jax.ShapeDtypeStruct((bsz * t, D_Z), _BF16),
            jax.ShapeDtypeStruct((bsz * t, D), _BF16),
            jax.ShapeDtypeStruct((bsz, CONV_W - 1, D), _F32),
            jax.ShapeDtypeStruct((bsz, 1, D), _F32),
        ],
        scratch_shapes=[
            pltpu.VMEM((N_LRU_BLOCKS, SUBLANES * _seg_pitch(tt // SUBLANES), LRU_BLOCK), _F32),
            pltpu.VMEM((tt, D), _F32),
            pltpu.VMEM((tt, D), _F32),
            pltpu.VMEM((CONV_W - 1, D), _F32),
            pltpu.VMEM((1, D), _F32),
        ],
        compiler_params=pltpu.CompilerParams(
            dimension_semantics=("arbitrary", "arbitrary"), vmem_limit_bytes=VMEM_LIMIT),
        name="mix_in",
    )(x2, mod3, w_in, hist, h0, conv_w, conv_b, wg, bg, lam)


def _attn_kernel(q_ref, kp_ref, kc_ref, vp_ref, vc_ref, sgb_ref, row0_ref, yb_ref, bias_scr,
                 sc_scr, *, tq, sb, band):
    i = pl.program_id(1)
    past = kp_ref.shape[0]
    win = past + sb
    nsub = tq // sb
    general = nsub if band else 0

    @pl.when(jnp.logical_and(pl.program_id(0) == 0, i == 0))
    def _build_bias():
        qq = lax.broadcasted_iota(jnp.int32, (sb, win), 0)
        kk = lax.broadcasted_iota(jnp.int32, (sb, win), 1)
        if band:
            dc = (kk // CHUNK) - (qq // CHUNK)
            ok = jnp.logical_and(dc >= 0, dc <= BAND_ROWS // CHUNK)
        else:
            ok = kk < past + CHUNK
        for h in range(N_HEADS):
            base = jnp.broadcast_to(row0_ref[h:h + 1, :], (sb, ROLL_W))
            toep = pltpu.roll(base, 0, 1, stride=1, stride_axis=0)[:, :win] * LOG2E
            gen = jnp.where(ok, toep, NEG_INF)
            bias_scr[general, h] = gen
            if band:
                for s in range(nsub):
                    bias_scr[s, h] = jnp.where(kk + s * sb >= past, gen, NEG_INF)

    def rows(prev_ref, cur_ref, so, cols):
        parts = []
        if so < past:
            parts.append(prev_ref[so:past, cols])
        parts.append(cur_ref[max(so - past, 0):so + sb, cols])
        return parts[0] if len(parts) == 1 else jnp.concatenate(parts, axis=0)

    def scores(n, s, h):
        cols = slice(h * HEAD_DIM, (h + 1) * HEAD_DIM)
        qh = q_ref[s * sb:(s + 1) * sb, cols]
        kh = rows(kp_ref, kc_ref, s * sb, cols)
        sc = lax.dot_general(qh, kh, (((1,), (1,)), ((), ())), preferred_element_type=_F32)
        sc_scr[n % SC_SLOTS] = sc + bias_scr[jnp.where(i == 0, s, general) if band else 0, h]

    def probs(n, s, h):
        sc = sc_scr[n % SC_SLOTS]
        p = jnp.exp2(sc - jnp.max(sc, axis=-1, keepdims=True))
        gate = sgb_ref[s * sb:(s + 1) * sb, h * HEAD_DIM:(h + 1) * HEAD_DIM].astype(_F32)
        return p.astype(_BF16), gate * (1.0 / jnp.sum(p, axis=-1, keepdims=True))

    def weighted_values(s, h, p, scale):
        cols = slice(h * HEAD_DIM, (h + 1) * HEAD_DIM)
        vh = rows(vp_ref, vc_ref, s * sb, cols)
        o = jnp.dot(p, vh, preferred_element_type=_F32)
        yb_ref[s * sb:(s + 1) * sb, cols] = (o * scale).astype(_BF16)

    pairs = [(s, h) for s in range(nsub) for h in range(N_HEADS)]
    for n in range(min(SC_AHEAD, len(pairs))):
        scores(n, *pairs[n])
    ready = probs(0, *pairs[0])
    for n, (s, h) in enumerate(pairs):
        if n + SC_AHEAD < len(pairs):
            scores(n + SC_AHEAD, *pairs[n + SC_AHEAD])
        current = ready
        if n + 1 < len(pairs):
            ready = probs(n + 1, *pairs[n + 1])
        weighted_values(s, h, *current)


def _attn(q_arr, q_col, kp_arr, kp_map, kc_arr, kc_col, vp_arr, vp_map, vc_arr, vc_col,
          sgb_arr, sgb_col, row0, bsz, t, tq, sb, past, band):
    nt = t // tq
    win = past + sb
    n_bias = tq // sb + 1 if band else 1
    kern = functools.partial(_attn_kernel, tq=tq, sb=sb, band=band)
    return pl.pallas_call(
        kern,
        grid=(bsz, nt),
        in_specs=[
            pl.BlockSpec((tq, D), lambda b, i: (b * nt + i, q_col)),
            pl.BlockSpec((past, D), kp_map),
            pl.BlockSpec((tq, D), lambda b, i: (b * nt + i, kc_col)),
            pl.BlockSpec((past, D), vp_map),
            pl.BlockSpec((tq, D), lambda b, i: (b * nt + i, vc_col)),
            pl.BlockSpec((tq, D), lambda b, i: (b * nt + i, sgb_col)),
            _resident((N_HEADS, ROLL_W)),
        ],
        out_specs=pl.BlockSpec((tq, D), lambda b, i: (b * nt + i, 0)),
        out_shape=jax.ShapeDtypeStruct((bsz * t, D), _BF16),
        scratch_shapes=[
            pltpu.VMEM((n_bias, N_HEADS, sb, win), _F32),
            pltpu.VMEM((SC_SLOTS, sb, win), _F32),
        ],
        compiler_params=pltpu.CompilerParams(
            dimension_semantics=("arbitrary", "arbitrary"), vmem_limit_bytes=VMEM_LIMIT),
        name="attn",
    )(q_arr, kp_arr, kc_arr, vp_arr, vc_arr, sgb_arr, row0)


def _final_kernel(x_ref, ya_ref, yb_ref, mod_ref, wo_ref, wu_ref, bu_ref, wd_ref, bd_ref,
                  ln_ref, y_ref):
    g1 = mod_ref[:, 2 * D:3 * D]
    sh2 = mod_ref[:, 3 * D:4 * D]
    sc2 = mod_ref[:, 4 * D:5 * D]
    g2 = mod_ref[:, 5 * D:6 * D]

    def row_group(rows):
        merged = (ya_ref[rows, :].astype(_F32) + yb_ref[rows, :].astype(_F32)).astype(_BF16)
        yield
        o = jnp.dot(merged, wo_ref[...], preferred_element_type=_F32)
        yield
        x1 = _layer_norm(ALPHA * x_ref[rows, :] + (1.0 + g1) * o, ln_ref[0:1, :], ln_ref[1:2, :])
        u2 = (x1 * (1.0 + sc2) + sh2).astype(_BF16)
        yield
        f = bd_ref[...]
        for c in range(D_FF // D):
            cols = slice(c * D, (c + 1) * D)
            hid = jnp.dot(u2, wu_ref[:, cols], preferred_element_type=_F32) + bu_ref[:, cols]
            hid = jnp.square(jnp.maximum(hid, 0.0)).astype(_BF16)
            f = f + jnp.dot(hid, wd_ref[cols, :], preferred_element_type=_F32)
            yield
        y_ref[rows, :] = _layer_norm(ALPHA * x1 + (1.0 + g2) * f, ln_ref[2:3, :], ln_ref[3:4, :])

    tm = x_ref.shape[0]
    n_groups = FINAL_ROW_GROUPS if tm >= FINAL_ROW_GROUPS * FINAL_MIN_GROUP_ROWS else 1
    gm = tm // n_groups
    groups = [row_group(slice(k * gm, (k + 1) * gm)) for k in range(n_groups)]
    live = list(range(n_groups))
    tick = 0
    while live:
        for k in list(live):
            if tick >= k:
                if next(groups[k], StopIteration) is StopIteration:
                    live.remove(k)
        tick += 1


def _final(x2, ya, yb, mod3, w_out, w_up, b_up, w_down, b_down, ln, tm, tiles_per_batch):
    m = x2.shape[0]
    row = lambda i: (i, 0)
    return pl.pallas_call(
        _final_kernel,
        grid=(m // tm,),
        in_specs=[
            pl.BlockSpec((tm, D), row),
            pl.BlockSpec((tm, D), row),
            pl.BlockSpec((tm, D), row),
            pl.BlockSpec((None, 1, 6 * D), lambda i: (i // tiles_per_batch, 0, 0)),
            _resident((D, D)),
            _resident((D, D_FF)),
            _resident((1, D_FF)),
            _resident((D_FF, D)),
            _resident((1, D)),
            _resident((4, D)),
        ],
        out_specs=pl.BlockSpec((tm, D), row),
        out_shape=jax.ShapeDtypeStruct((m, D), _F32),
        compiler_params=pltpu.CompilerParams(
            dimension_semantics=("arbitrary",), vmem_limit_bytes=VMEM_LIMIT),
        name="final",
    )(x2, ya, yb, mod3, w_out, w_up, b_up, w_down, b_down, ln)


def _rel_bias_row0(table, past):
    far = jnp.broadcast_to(table[:, 2 * MAX_REL:], (N_HEADS, past - MAX_REL + 1))
    near = table[:, 2 * MAX_REL - 1::-1]
    tail = jnp.broadcast_to(table[:, 2 * MAX_REL:],
                            (N_HEADS, ROLL_W - (past + MAX_REL + 1)))
    return jnp.concatenate([far, near, tail], axis=1).astype(_F32)


def _heads(a, bsz, rows):
    return a.reshape(1, bsz, rows, N_HEADS, HEAD_DIM)


@jax.jit
def _forward(x_prompt, x_sample, c_prompt, c_sample, cache_k, cache_v, state_conv, state_lru,
             w_ada, b_ada, w_in, conv_w, conv_b, w_rg, b_rg, w_ig, b_ig, lru_lambda, rel_bias,
             w_out, ln1_g, ln1_b, w_up, b_up, w_down, b_down, ln2_g, ln2_b):
    bp, t, _ = x_prompt.shape
    bs, s, _ = x_sample.shape
    l = 0
    col_scale = jnp.where(jnp.arange(D_IN) >= COL_GA * D, 0.5, 1.0).astype(_F32)
    w_in_b = (w_in[l] * col_scale[None, :]).astype(_BF16)
    w_out_b = w_out[l].astype(_BF16)
    w_up_b = w_up[l].astype(_BF16)
    w_down_b = w_down[l].astype(_BF16)
    wg =(0.5 * jnp.concatenate([w_rg[l], w_ig[l]], axis=-1)).astype(_BF16)
    bg = 0.5 * jnp.concatenate([b_rg[l], b_ig[l]], axis=-1)[:, None, :]
    lam = lru_lambda[l][None, :]
    cb = conv_b[l][None, :]
    ln = jnp.stack([ln1_g[l], ln1_b[l], ln2_g[l], ln2_b[l]])
    bu = b_up[l][None, :]
    bd = b_down[l][None, :]

    mod = _ada(jnp.concatenate([c_prompt, c_sample], axis=0), w_ada[l].astype(_BF16),
               b_ada[l][None, :])
    mod_p = mod[:bp, None, :]
    mod_s = mod[bp:, None, :]

    tm = 512
    xp2 = x_prompt.reshape(bp * t, D)
    z, ya, conv_p, h_p = _mix_in(xp2, mod_p, w_in_b, jnp.zeros((bp, CONV_W - 1, D), _F32),
                                 jnp.zeros((bp, 1, D), _F32), conv_w[l], cb, wg, bg, lam,
                                 bp, t, tm, True)
    tq, sb = 512, 128
    nt = t // tq
    row0 = _rel_bias_row0(rel_bias[l], BAND_ROWS)
    prev_k = lambda b, i: (b * nt + jnp.maximum(i - 1, 0), Z_K)
    prev_v = lambda b, i: (b * nt + jnp.maximum(i - 1, 0), Z_V)
    yb = _attn(z, Z_Q, z, prev_k, z, Z_K, z, prev_v, z, Z_V, z, Z_GB, row0,
               bp, t, tq, sb, BAND_ROWS, True)
    y_p = _final(xp2, ya, yb, mod_p, w_out_b, w_up_b, bu, w_down_b, bd, ln, tm, t // tm)
    z3 = z.reshape(bp, t, D_Z)
    rows = min(BAND_ROWS, t)
    k_p = z3[:, t - rows:, Z_K * D:(Z_K + 1) * D].astype(_F32)
    v_p = z3[:, t - rows:, Z_V * D:(Z_V + 1) * D].astype(_F32)

    xs2 = x_sample.reshape(bs * s, D)
    zs, ya_s, conv_s, h_s = _mix_in(xs2, mod_s, w_in_b, state_conv[l], state_lru[l][:, None, :],
                                    conv_w[l], cb, wg, bg, lam, bs, s, s, False)
    sbs = 2 * CHUNK
    zs3 = zs.reshape(bs, s, D_Z)
    zs_pad = jnp.pad(zs3, ((0, 0), (0, sbs - s), (0, 0))).reshape(bs * sbs, D_Z)
    n_past = cache_k.shape[2]
    ck = cache_k[l].reshape(bs * n_past, D).astype(_BF16)
    cv = cache_v[l].reshape(bs * n_past, D).astype(_BF16)
    cache_map = lambda b, i: (b, 0)
    yb_s = _attn(zs_pad, Z_Q, ck, cache_map, zs_pad, Z_K, cv, cache_map, zs_pad, Z_V,
                 zs_pad, Z_GB, _rel_bias_row0(rel_bias[l], n_past),
                 bs, sbs, sbs, sbs, n_past, False)
    yb_s = yb_s.reshape(bs, sbs, D)[:, :s].reshape(bs * s, D)
    y_s = _final(xs2, ya_s, yb_s, mod_s, w_out_b, w_up_b, bu, w_down_b, bd, ln, s, 1)
    k_s = zs3[:, :, Z_K * D:(Z_K + 1) * D].astype(_F32)
    v_s = zs3[:, :, Z_V * D:(Z_V + 1) * D].astype(_F32)

    return (y_p.reshape(bp, t, D), y_s.reshape(bs, s, D),
            _heads(k_p, bp, rows), _heads(v_p, bp, rows), conv_p[None], h_p.reshape(1, bp, D),
            _heads(k_s, bs, s), _heads(v_s, bs, s), conv_s[None], h_s.reshape(1, bs, D))


def kernel(x_prompt, x_sample, c_prompt, c_sample, cache_k, cache_v, state_conv, state_lru, w_ada, b_ada, w_in, conv_w, conv_b, w_rg, b_rg, w_ig, b_ig, lru_lambda, rel_bias, w_out, ln1_g, ln1_b, w_up, b_up, w_down, b_down, ln2_g, ln2_b):
    return _forward(x_prompt, x_sample, c_prompt, c_sample, cache_k, cache_v, state_conv,
                    state_lru, w_ada, b_ada, w_in, conv_w, conv_b, w_rg, b_rg, w_ig, b_ig,
                    lru_lambda, rel_bias, w_out, ln1_g, ln1_b, w_up, b_up, w_down, b_down,
                    ln2_g, ln2_b)
```

```python
import functools
import math

import jax
import jax.numpy as jnp
from jax import lax
from jax.experimental import pallas as pl
from jax.experimental.pallas import tpu as pltpu

D = 1024
CHUNK = 64
BAND_ROWS = 8 * CHUNK
N_HEADS = 8
HEAD_DIM = D // N_HEADS
N_LRU_BLOCKS = 8
LRU_BLOCK = D // N_LRU_BLOCKS
CONV_W = 4
LRU_C = 8.0
MAX_REL = 128
ATT_SCALE = HEAD_DIM ** -0.5
NEG_INF = -1e30
D_FF = 4 * D
D_IN = 7 * D
DEPTH = 1
ALPHA = (2 * DEPTH) ** 0.25
LN_EPS = 1e-5

LOG2E = math.log2(math.e)
Q_SCALE = ATT_SCALE * LOG2E
ROLL_W = 768
SC_AHEAD = 2
SC_SLOTS = SC_AHEAD + 1
GELU_C1 = math.sqrt(2.0 / math.pi)
GELU_C2 = GELU_C1 * 0.044715
SQRT_FLOOR = 1e-30

SUBLANES = 8
VMEM_LIMIT = 56 * 1024 * 1024
VMEM_LIMIT_ATTN_OUT = 62 * 1024 * 1024

COL_XR, COL_GL, COL_Q, COL_K, COL_V, COL_GA, COL_GB = range(7)
Z_SOURCE_COLS = (COL_Q, COL_K, COL_V, COL_GB)
Z_Q, Z_K, Z_V, Z_GB = range(4)
D_Z = len(Z_SOURCE_COLS) * D
PROJ_W = 256
FINAL_GROUP_PARTS = (1, 1)
FINAL_MIN_PART_ROWS = 256
Z_ROW_SPLIT = 2
Z_MIN_ROWS = 256
LRU_STAGES = 7

FF_CHUNK = 512

_F32 = jnp.float32
_BF16 = jnp.bfloat16
_DONE = object()


def _resident(shape):
    n = len(shape)
    return pl.BlockSpec(shape, lambda *_: (0,) * n, pipeline_mode=pl.Buffered(1))


def _sigmoid(v):
    return 0.5 * jnp.tanh(0.5 * v) + 0.5


def _layer_norm(v, g, b):
    mu = jnp.mean(v, axis=-1, keepdims=True)
    vc = v - mu
    var = jnp.mean(vc * vc, axis=-1, keepdims=True)
    return vc * lax.rsqrt(var + LN_EPS) * g + b


def _ada_kernel(c_ref, w_ref, b_ref, o_ref):
    c = c_ref[...]
    s = (c * _sigmoid(c)).astype(_BF16)
    o_ref[...] = jnp.dot(s, w_ref[...], preferred_element_type=_F32) + b_ref[...]


def _ada(c, w_ada, b_ada):
    n = c.shape[0]
    return pl.pallas_call(
        _ada_kernel,
        out_shape=jax.ShapeDtypeStruct((n, 6 * D), _F32),
        compiler_params=pltpu.CompilerParams(vmem_limit_bytes=VMEM_LIMIT),
        name="ada",
    )(c, w_ada, b_ada)


def _seg_pitch(seg):
    return seg if (seg // SUBLANES) % 2 == 1 else seg + SUBLANES


def _mix_in_kernel(x_ref, mod_ref, w_ref, hist_ref, h0_ref, cw_ref, cb_ref, wg_ref, bg_ref,
                   lam_ref, z_ref, ya_ref, conv_ref, hlast_ref, stage, gl_scr, ga_scr, hist_scr,
                   hc_scr, *, tt, reset_first):
    i = pl.program_id(1)
    seg = tt // SUBLANES
    pitch = _seg_pitch(seg)
    n_hist = CONV_W - 1

    @pl.when(i == 0)
    def _():
        hist_scr[...] = hist_ref[...]
        hc_scr[...] = h0_ref[...]

    lam = lam_ref[...]
    half_coef = (-0.5 * LRU_C * LOG2E) * (jnp.maximum(-lam, 0.0)
                                         + jnp.log1p(jnp.exp(-jnp.abs(lam))))
    srow = lax.broadcasted_iota(jnp.int32, (SUBLANES, LRU_BLOCK), 0)

    sh1 = mod_ref[:, 0:D]
    sc1 = mod_ref[:, D:2 * D]
    u = (x_ref[...] * (1.0 + sc1) + sh1).astype(_BF16)

    def project(c0):
        return jnp.dot(u, w_ref[:, c0:c0 + PROJ_W], preferred_element_type=_F32)

    def project_rnn(k):
        c0 = k * PROJ_W
        xr = project(COL_XR * D + c0)
        conv_ref[:, c0:c0 + PROJ_W] = xr[tt - n_hist:tt, :]
        for m in range(PROJ_W // LRU_BLOCK):
            for s in range(SUBLANES):
                stage[k * (PROJ_W // LRU_BLOCK) + m, s * pitch:s * pitch + seg, :] = (
                    xr[s * seg:(s + 1) * seg, m * LRU_BLOCK:(m + 1) * LRU_BLOCK])
        gl_scr[:, c0:c0 + PROJ_W] = project(COL_GL * D + c0)
        ga_scr[:, c0:c0 + PROJ_W] = project(COL_GA * D + c0)

    z_rows = tt // Z_ROW_SPLIT if tt >= Z_ROW_SPLIT * Z_MIN_ROWS else tt

    def project_z(k):
        kc, kr = divmod(k, tt // z_rows)
        col, off = divmod(kc * PROJ_W, D)
        c0 = Z_SOURCE_COLS[col] * D + off
        rows = slice(kr * z_rows, (kr + 1) * z_rows)
        zk = jnp.dot(u[rows], w_ref[:, c0:c0 + PROJ_W], preferred_element_type=_F32)
        if Z_SOURCE_COLS[col] == COL_Q:
            zk = zk * Q_SCALE
        elif Z_SOURCE_COLS[col] == COL_GB:
            zk = 0.5 * jnp.tanh(zk) + 0.5
        z_ref[rows, kc * PROJ_W:(kc + 1) * PROJ_W] = zk.astype(_BF16)

    def step_regs(v):
        return [v[g * SUBLANES:(g + 1) * SUBLANES] for g in range(seg)]

    def lru_block(n):
        cols = slice(n * LRU_BLOCK, (n + 1) * LRU_BLOCK)
        xg = [stage[n, pl.ds(g, SUBLANES, stride=pitch), :] for g in range(seg)]
        xt = jnp.concatenate(xg, axis=0)

        pre = []
        for j in range(1, n_hist + 1):
            hrow = jnp.broadcast_to(hist_scr[n_hist - j:n_hist - j + 1, cols], (SUBLANES, LRU_BLOCK))
            pre.append(jnp.where(srow == 0, hrow, pltpu.roll(xg[seg - j], 1, 0)))

        def back(j):
            head = [pre[k - 1] for k in range(j, 0, -1)]
            return jnp.concatenate(head + [xt[:tt - j * SUBLANES]], axis=0)

        xc = cb_ref[:, cols] + back(n_hist) * cw_ref[0:1, cols]
        for j in range(1, n_hist):
            xc = xc + back(n_hist - j) * cw_ref[j:j + 1, cols]
        xc = xc + xt * cw_ref[n_hist:CONV_W, cols]

        yield
        gates = jnp.dot(xc.astype(_BF16), wg_ref[n], preferred_element_type=_F32) + bg_ref[n]
        hc = half_coef[:, cols]
        a = jnp.exp2(hc * jnp.tanh(gates[:, :LRU_BLOCK]) + hc)
        mult = jnp.exp2(0.5 * jnp.log2(1.0 - a * a))
        half_xc = 0.5 * xc
        ix = half_xc * jnp.tanh(gates[:, LRU_BLOCK:]) + half_xc
        av = step_regs(a)
        bv = step_regs(mult * ix)
        if reset_first:
            first = jnp.logical_and(srow == 0, i == 0)
            av[0] = jnp.where(first, 0.0, av[0])
            bv[0] = jnp.where(first, ix[:SUBLANES], bv[0])

        yield
        seg_a, seg_b = av[0], bv[0]
        for g in range(1, seg):
            seg_b = av[g] * seg_b + bv[g]
            seg_a = av[g] * seg_a
            if g == seg // 2:
                yield
        for d in (1, 2, 4):
            keep = srow >= d
            a_sh = jnp.where(keep, pltpu.roll(seg_a, d, 0), 1.0)
            b_sh = jnp.where(keep, pltpu.roll(seg_b, d, 0), 0.0)
            seg_b = seg_a * b_sh + seg_b
            seg_a = seg_a * a_sh
        yield
        h_in = jnp.broadcast_to(hc_scr[:, cols], (SUBLANES, LRU_BLOCK))
        h_end = seg_a * h_in + seg_b
        h = jnp.where(srow == 0, h_in, pltpu.roll(h_end, 1, 0))
        for g in range(seg):
            h = av[g] * h + bv[g]
            stage[n, pl.ds(g, SUBLANES, stride=pitch), :] = h
            if g == seg // 2:
                yield
        hc_scr[:, cols] = h[SUBLANES - 1:SUBLANES]
        hlast_ref[:, cols] = h[SUBLANES - 1:SUBLANES]

        yield
        for s in range(SUBLANES):
            rows = slice(s * seg, (s + 1) * seg)
            gl = gl_scr[rows, cols]
            hq = stage[n, s * pitch:s * pitch + seg, :] * (0.25 * gl)
            w = hq + hq * jnp.tanh(gl * (GELU_C1 + GELU_C2 * (gl * gl)))
            ya_ref[rows, cols] = (w + w * jnp.tanh(ga_scr[rows, cols])).astype(_BF16)
            if s == SUBLANES // 2 - 1:
                yield

    blocks_per_chunk = PROJ_W // LRU_BLOCK
    n_z = (D_Z // PROJ_W) * (tt // z_rows)
    n_slots = N_LRU_BLOCKS * LRU_STAGES
    slot = 0
    project_rnn(0)
    for n in range(N_LRU_BLOCKS):
        if n % blocks_per_chunk == 0 and n + blocks_per_chunk < N_LRU_BLOCKS:
            project_rnn(n // blocks_per_chunk + 1)
        for _ in lru_block(n):
            for k in range(slot * n_z // n_slots, (slot + 1) * n_z // n_slots):
                project_z(k)
            slot += 1
    assert slot == n_slots

    hist_scr[...] = conv_ref[...]


def _mix_in(x2, mod3, w_in, hist, h0, conv_w, conv_b, wg, bg, lam, bsz, t, tt, reset_first):
    nt = t // tt
    kern = functools.partial(_mix_in_kernel, tt=tt, reset_first=reset_first)
    return pl.pallas_call(
        kern,
        grid=(bsz, nt),
        in_specs=[
            pl.BlockSpec((tt, D), lambda b, i: (b * nt + i, 0)),
            pl.BlockSpec((None, 1, 6 * D), lambda b, i: (b, 0, 0)),
            _resident((D, D_IN)),
            pl.BlockSpec((None, CONV_W - 1, D), lambda b, i: (b, 0, 0)),
            pl.BlockSpec((None, 1, D), lambda b, i: (b, 0, 0)),
            _resident((CONV_W, D)),
            _resident((1, D)),
            _resident((N_LRU_BLOCKS, LRU_BLOCK, 2 * LRU_BLOCK)),
            _resident((N_LRU_BLOCKS, 1, 2 * LRU_BLOCK)),
            _resident((1, D)),
        ],
        out_specs=[
            pl.BlockSpec((tt, D_Z), lambda b, i: (b * nt + i, 0)),
            pl.BlockSpec((tt, D), lambda b, i: (b * nt + i, 0)),
            pl.BlockSpec((None, CONV_W - 1, D), lambda b, i: (b, 0, 0)),
            pl.BlockSpec((None, 1, D), lambda b, i: (b, 0, 0)),
        ],
        out_shape=[
            jax.ShapeDtypeStruct((bsz * t, D_Z), _BF16),
            jax.ShapeDtypeStruct((bsz * t, D), _BF16),
            jax.ShapeDtypeStruct((bsz, CONV_W - 1, D), _F32),
            jax.ShapeDtypeStruct((bsz, 1, D), _F32),
        ],
        scratch_shapes=[
            pltpu.VMEM((N_LRU_BLOCKS, SUBLANES * _seg_pitch(tt // SUBLANES), LRU_BLOCK), _F32),
            pltpu.VMEM((tt, D), _F32),
            pltpu.VMEM((tt, D), _F32),
            pltpu.VMEM((CONV_W - 1, D), _F32),
            pltpu.VMEM((1, D), _F32),
        ],
        compiler_params=pltpu.CompilerParams(
            dimension_semantics=("arbitrary", "arbitrary"), vmem_limit_bytes=VMEM_LIMIT),
        name="mix_in",
    )(x2, mod3, w_in, hist, h0, conv_w, conv_b, wg, bg, lam)


def _attn_out_kernel(q_ref, kp_ref, kc_ref, vp_ref, vc_ref, sgb_ref, row0_ref,
                     x_ref, ya_ref, mod_ref, wo_ref, wu_ref, bu_ref, wd_ref, bd_ref, ln_ref,
                     y_ref, bias_scr, sc_scr, yb_scr, *, tm, sb, band, nt):
    j = pl.program_id(0)
    i = jnp.minimum(j, pl.num_programs(0) - 2) % nt
    past = kp_ref.shape[0]
    win = past + sb
    nsub = tm // sb
    slot_w = j % 2
    slot_r = 1 - slot_w

    @pl.when(j == 0)
    def _first_step():
        yb_scr[...] = jnp.zeros_like(yb_scr)
        qq = lax.broadcasted_iota(jnp.int32, (sb, win), 0)
        kk = lax.broadcasted_iota(jnp.int32, (sb, win), 1)
        if band:
            dc = (kk // CHUNK) - (qq // CHUNK)
            ok = jnp.logical_and(dc >= 0, dc <= BAND_ROWS // CHUNK)
        else:
            ok = kk < past + CHUNK
        for h in range(N_HEADS):
            base = jnp.broadcast_to(row0_ref[h:h + 1, :], (sb, ROLL_W))
            toep = pltpu.roll(base, 0, 1, stride=1, stride_axis=0)[:, :win] * LOG2E
            bias_scr[h] = jnp.where(ok, toep, NEG_INF)

    start_pen = jnp.where(i == 0, NEG_INF, 0.0) if band else None

    def rows(prev_ref, cur_ref, so, cols):
        parts = []
        if so < past:
            parts.append(prev_ref[so:past, cols])
        parts.append(cur_ref[max(so - past, 0):so + sb, cols])
        return parts[0] if len(parts) == 1 else jnp.concatenate(parts, axis=0)

    def scores(n, s, h):
        cols = slice(h * HEAD_DIM, (h + 1) * HEAD_DIM)
        qh = q_ref[s * sb:(s + 1) * sb, cols]
        kh = rows(kp_ref, kc_ref, s * sb, cols)
        sc = lax.dot_general(qh, kh, (((1,), (1,)), ((), ())), preferred_element_type=_F32)
        sc = sc + bias_scr[h]
        before_start = past - s * sb
        if band and before_start > 0:
            sc = jnp.concatenate([sc[:, :before_start] + start_pen, sc[:, before_start:]], axis=1)
        sc_scr[n % SC_SLOTS] = sc

    def probs(n, s, h):
        sc = sc_scr[n % SC_SLOTS]
        p = jnp.exp2(sc - jnp.max(sc, axis=-1, keepdims=True))
        gate = sgb_ref[s * sb:(s + 1) * sb, h * HEAD_DIM:(h + 1) * HEAD_DIM].astype(_F32)
        return p.astype(_BF16), gate * (1.0 / jnp.sum(p, axis=-1, keepdims=True))

    def weighted_values(s, h, p, scale):
        cols = slice(h * HEAD_DIM, (h + 1) * HEAD_DIM)
        vh = rows(vp_ref, vc_ref, s * sb, cols)
        o = jnp.dot(p, vh, preferred_element_type=_F32)
        yb_scr[slot_w, s * sb:(s + 1) * sb, cols] = (o * scale).astype(_BF16)

    def attention_stages():
        pairs = [(s, h) for s in range(nsub) for h in range(N_HEADS)]
        for n in range(min(SC_AHEAD, len(pairs))):
            scores(n, *pairs[n])
        ready = probs(0, *pairs[0])
        for n, (s, h) in enumerate(pairs):
            yield
            if n + SC_AHEAD < len(pairs):
                scores(n + SC_AHEAD, *pairs[n + SC_AHEAD])
            current = ready
            if n + 1 < len(pairs):
                ready = probs(n + 1, *pairs[n + 1])
            weighted_values(s, h, *current)

    g1 = mod_ref[:, 2 * D:3 * D]
    sh2 = mod_ref[:, 3 * D:4 * D]
    sc2 = mod_ref[:, 4 * D:5 * D]
    g2 = mod_ref[:, 5 * D:6 * D]

    def token_stages(rows):
        yb = yb_scr[slot_r, rows, :]
        merged = (ya_ref[rows, :].astype(_F32) + yb.astype(_F32)).astype(_BF16)
        yield
        o = jnp.dot(merged, wo_ref[...], preferred_element_type=_F32)
        yield
        x1 = _layer_norm(ALPHA * x_ref[rows, :] + (1.0 + g1) * o, ln_ref[0:1, :], ln_ref[1:2, :])
        u2 = (x1 * (1.0 + sc2) + sh2).astype(_BF16)
        yield
        f = bd_ref[...]
        for c in range(D_FF // FF_CHUNK):
            cols = slice(c * FF_CHUNK, (c + 1) * FF_CHUNK)
            hid = jnp.dot(u2, wu_ref[:, cols], preferred_element_type=_F32) + bu_ref[:, cols]
            hid = jnp.square(jnp.maximum(hid, 0.0)).astype(_BF16)
            yield
            f = f + jnp.dot(hid, wd_ref[cols, :], preferred_element_type=_F32)
            yield
        y_ref[rows, :] = _layer_norm(ALPHA * x1 + (1.0 + g2) * f, ln_ref[2:3, :], ln_ref[3:4, :])

    def staggered(gens):
        live = list(range(len(gens)))
        tick = 0
        while True:
            for k in list(live):
                if tick >= k and next(gens[k], _DONE) is _DONE:
                    live.remove(k)
            tick += 1
            if not live:
                return
            yield

    parts = FINAL_GROUP_PARTS if tm >= sum(FINAL_GROUP_PARTS) * FINAL_MIN_PART_ROWS else (1,)
    unit = tm // sum(parts)
    bounds = [unit * sum(parts[:k]) for k in range(len(parts) + 1)]
    tokens = staggered([token_stages(slice(bounds[k], bounds[k + 1]))
                        for k in range(len(parts))])
    attention = attention_stages()
    n_att = nsub * N_HEADS + 1
    n_tok = len(parts) - 1 + 4 + 2 * (D_FF // FF_CHUNK)
    done_att = done_tok = 0
    while done_att < n_att or done_tok < n_tok:
        if done_tok == n_tok or (done_att < n_att and done_att * n_tok <= done_tok * n_att):
            next(attention, _DONE)
            done_att += 1
        else:
            next(tokens, _DONE)
            done_tok += 1
    assert next(attention, _DONE) is _DONE and next(tokens, _DONE) is _DONE


def _attn_out(z, kp_arr, vp_arr, sample_cache, row0, x2, ya, mod3, w_out, w_up, b_up, w_down,
              b_down, ln, bsz, t, tm, sb, past, band):
    nt = t // tm
    n_tiles = bsz * nt
    win = past + sb
    kern = functools.partial(_attn_out_kernel, tm=tm, sb=sb, band=band, nt=nt)

    def att_tile(j):
        return jnp.minimum(j, n_tiles - 1)

    def tok_tile(j):
        return jnp.maximum(j - 1, 0)

    def prev_rows(col):
        if sample_cache:
            return lambda j: (att_tile(j), 0)
        return lambda j: (jnp.where(att_tile(j) % nt == 0, att_tile(j), att_tile(j) - 1), col)

    return pl.pallas_call(
        kern,
        grid=(n_tiles + 1,),
        in_specs=[
            pl.BlockSpec((tm, D), lambda j: (att_tile(j), Z_Q)),
            pl.BlockSpec((past, D), prev_rows(Z_K)),
            pl.BlockSpec((tm, D), lambda j: (att_tile(j), Z_K)),
            pl.BlockSpec((past, D), prev_rows(Z_V)),
            pl.BlockSpec((tm, D), lambda j: (att_tile(j), Z_V)),
            pl.BlockSpec((tm, D), lambda j: (att_tile(j), Z_GB)),
            _resident((N_HEADS, ROLL_W)),
            pl.BlockSpec((tm, D), lambda j: (tok_tile(j), 0)),
            pl.BlockSpec((tm, D), lambda j: (tok_tile(j), 0)),
            pl.BlockSpec((None, 1, 6 * D), lambda j: (tok_tile(j) // nt, 0, 0)),
            _resident((D, D)),
            _resident((D, D_FF)),
            _resident((1, D_FF)),
            _resident((D_FF, D)),
            _resident((1, D)),
            _resident((4, D)),
        ],
        out_specs=pl.BlockSpec((tm, D), lambda j: (tok_tile(j), 0)),
        out_shape=jax.ShapeDtypeStruct((bsz * t, D), _F32),
        scratch_shapes=[
            pltpu.VMEM((N_HEADS, sb, win), _F32),
            pltpu.VMEM((SC_SLOTS, sb, win), _F32),
            pltpu.VMEM((2, tm, D), _BF16),
        ],
        compiler_params=pltpu.CompilerParams(
            dimension_semantics=("arbitrary",), vmem_limit_bytes=VMEM_LIMIT_ATTN_OUT),
        name="attn_out",
    )(z, kp_arr, z, vp_arr, z, z, row0, x2, ya, mod3, w_out, w_up, b_up, w_down, b_down, ln)


def _rel_bias_row0(table, past):
    far = jnp.broadcast_to(table[:, 2 * MAX_REL:], (N_HEADS, past - MAX_REL + 1))
    near = table[:, 2 * MAX_REL - 1::-1]
    tail = jnp.broadcast_to(table[:, 2 * MAX_REL:],
                            (N_HEADS, ROLL_W - (past + MAX_REL + 1)))
    return jnp.concatenate([far, near, tail], axis=1).astype(_F32)


def _heads(a, bsz, rows):
    return a.reshape(1, bsz, rows, N_HEADS, HEAD_DIM)


@jax.jit
def _forward(x_prompt, x_sample, c_prompt, c_sample, cache_k, cache_v, state_conv, state_lru,
             w_ada, b_ada, w_in, conv_w, conv_b, w_rg, b_rg, w_ig, b_ig, lru_lambda, rel_bias,
             w_out, ln1_g, ln1_b, w_up, b_up, w_down, b_down, ln2_g, ln2_b):
    bp, t, _ = x_prompt.shape
    bs, s, _ = x_sample.shape
    l = 0
    col_scale = jnp.where(jnp.arange(D_IN) >= COL_GA * D, 0.5, 1.0).astype(_F32)
    w_in_b = (w_in[l] * col_scale[None, :]).astype(_BF16)
    w_out_b = w_out[l].astype(_BF16)
    w_up_b = w_up[l].astype(_BF16)
    w_down_b = w_down[l].astype(_BF16)
    wg =(0.5 * jnp.concatenate([w_rg[l], w_ig[l]], axis=-1)).astype(_BF16)
    bg = 0.5 * jnp.concatenate([b_rg[l], b_ig[l]], axis=-1)[:, None, :]
    lam = lru_lambda[l][None, :]
    cb = conv_b[l][None, :]
    ln = jnp.stack([ln1_g[l], ln1_b[l], ln2_g[l], ln2_b[l]])
    bu = b_up[l][None, :]
    bd = b_down[l][None, :]

    mod = _ada(jnp.concatenate([c_prompt, c_sample], axis=0), w_ada[l].astype(_BF16),
               b_ada[l][None, :])
    mod_p = mod[:bp, None, :]
    mod_s = mod[bp:, None, :]

    tm = 512
    xp2 = x_prompt.reshape(bp * t, D)
    z, ya, conv_p, h_p = _mix_in(xp2, mod_p, w_in_b, jnp.zeros((bp, CONV_W - 1, D), _F32),
                                 jnp.zeros((bp, 1, D), _F32), conv_w[l], cb, wg, bg, lam,
                                 bp, t, tm, True)
    y_p = _attn_out(z, z, z, False, _rel_bias_row0(rel_bias[l], BAND_ROWS), xp2, ya, mod_p,
                    w_out_b, w_up_b, bu, w_down_b, bd, ln, bp, t, tm, 2 * CHUNK, BAND_ROWS, True)
    z3 = z.reshape(bp, t, D_Z)
    rows = min(BAND_ROWS, t)
    k_p = z3[:, t - rows:, Z_K * D:(Z_K + 1) * D].astype(_F32)
    v_p = z3[:, t - rows:, Z_V * D:(Z_V + 1) * D].astype(_F32)

    xs2 = x_sample.reshape(bs * s, D)
    zs, ya_s, conv_s, h_s = _mix_in(xs2, mod_s, w_in_b, state_conv[l], state_lru[l][:, None, :],
                                    conv_w[l], cb, wg, bg, lam, bs, s, s, False)
    sbs = 2 * CHUNK
    zs3 = zs.reshape(bs, s, D_Z)

    def pad_rows(a):
        a3 = a.reshape(bs, s, a.shape[-1])
        return jnp.pad(a3, ((0, 0), (0, sbs - s), (0, 0))).reshape(bs * sbs, a.shape[-1])

    n_past = cache_k.shape[2]
    ck = cache_k[l].reshape(bs * n_past, D).astype(_BF16)
    cv = cache_v[l].reshape(bs * n_past, D).astype(_BF16)
    y_s = _attn_out(pad_rows(zs), ck, cv, True, _rel_bias_row0(rel_bias[l], n_past),
                    pad_rows(xs2), pad_rows(ya_s), mod_s, w_out_b, w_up_b, bu, w_down_b, bd, ln,
                    bs, sbs, sbs, sbs, n_past, False)
    y_s = y_s.reshape(bs, sbs, D)[:, :s]
    k_s = zs3[:, :, Z_K * D:(Z_K + 1) * D].astype(_F32)
    v_s = zs3[:, :, Z_V * D:(Z_V + 1) * D].astype(_F32)

    return (y_p.reshape(bp, t, D), y_s.reshape(bs, s, D),
            _heads(k_p, bp, rows), _heads(v_p, bp, rows), conv_p[None], h_p.reshape(1, bp, D),
            _heads(k_s, bs, s), _heads(v_s, bs, s), conv_s[None], h_s.reshape(1, bs, D))


def kernel(x_prompt, x_sample, c_prompt, c_sample, cache_k, cache_v, state_conv, state_lru, w_ada, b_ada, w_in, conv_w, conv_b, w_rg, b_rg, w_ig, b_ig, lru_lambda, rel_bias, w_out, ln1_g, ln1_b, w_up, b_up, w_down, b_down, ln2_g, ln2_b):
    return _forward(x_prompt, x_sample, c_prompt, c_sample, cache_k, cache_v, state_conv,
                    state_lru, w_ada, b_ada, w_in, conv_w, conv_b, w_rg, b_rg, w_ig, b_ig,
                    lru_lambda, rel_bias, w_out, ln1_g, ln1_b, w_up, b_up, w_down, b_down,
                    ln2_g, ln2_b)
```

```python
import functools
import math

import jax
import jax.numpy as jnp
from jax import lax
from jax.experimental import pallas as pl
from jax.experimental.pallas import tpu as pltpu

D = 1024
CHUNK = 64
BAND_ROWS = 8 * CHUNK
N_HEADS = 8
HEAD_DIM = D // N_HEADS
N_LRU_BLOCKS = 8
LRU_BLOCK = D // N_LRU_BLOCKS
CONV_W = 4
LRU_C = 8.0
MAX_REL = 128
ATT_SCALE = HEAD_DIM ** -0.5
NEG_INF = -1e30
D_FF = 4 * D
D_IN = 7 * D
DEPTH = 1
ALPHA = (2 * DEPTH) ** 0.25
LN_EPS = 1e-5

LOG2E = math.log2(math.e)
Q_SCALE = ATT_SCALE * LOG2E
ROLL_W = 768
SC_AHEAD = 2
SC_SLOTS = SC_AHEAD + 1
GELU_C1 = math.sqrt(2.0 / math.pi)
GELU_C2 = GELU_C1 * 0.044715

SUBLANES = 8
VMEM_LIMIT = 56 * 1024 * 1024
VMEM_LIMIT_ATTN_OUT = 62 * 1024 * 1024

COL_XR, COL_GL, COL_Q, COL_K, COL_V, COL_GA, COL_GB = range(7)
Z_SOURCE_COLS = (COL_Q, COL_K, COL_V, COL_GB)
Z_Q, Z_K, Z_V, Z_GB = range(4)
D_Z = len(Z_SOURCE_COLS) * D
PROJ_W = 256
Z_ROW_SPLIT = 2
Z_MIN_ROWS = 256
LRU_STAGES = 7
TOKEN_GROUP_PARTS = (1, 1)
TOKEN_MIN_PART_ROWS = 256
FF_CHUNK = 512

_F32 = jnp.float32
_BF16 = jnp.bfloat16
_DONE = object()


def _resident(shape):
    n = len(shape)
    return pl.BlockSpec(shape, lambda *_: (0,) * n, pipeline_mode=pl.Buffered(1))


def _sigmoid(v):
    return 0.5 * jnp.tanh(0.5 * v) + 0.5


def _layer_norm(v, g, b):
    mu = jnp.mean(v, axis=-1, keepdims=True)
    vc = v - mu
    var = jnp.mean(vc * vc, axis=-1, keepdims=True)
    return vc * lax.rsqrt(var + LN_EPS) * g + b


def _ada_kernel(c_ref, w_ref, b_ref, o_ref):
    c = c_ref[...]
    s = (c * _sigmoid(c)).astype(_BF16)
    o_ref[...] = jnp.dot(s, w_ref[...], preferred_element_type=_F32) + b_ref[...]


def _ada(c, w_ada, b_ada):
    n = c.shape[0]
    return pl.pallas_call(
        _ada_kernel,
        out_shape=jax.ShapeDtypeStruct((n, 6 * D), _F32),
        compiler_params=pltpu.CompilerParams(vmem_limit_bytes=VMEM_LIMIT),
        name="ada",
    )(c, w_ada, b_ada)


def _seg_pitch(seg):
    return seg if (seg // SUBLANES) % 2 == 1 else seg + SUBLANES


def _mix_in_kernel(x_ref, mod_ref, w_ref, hist_ref, h0_ref, cw_ref, cb_ref, wg_ref, bg_ref,
                   lam_ref, z_ref, ya_ref, conv_ref, hlast_ref, stage, gl_scr, ga_scr, hist_scr,
                   hc_scr, *, tt, reset_first):
    i = pl.program_id(1)
    seg = tt // SUBLANES
    pitch = _seg_pitch(seg)
    n_hist = CONV_W - 1

    @pl.when(i == 0)
    def _():
        hist_scr[...] = hist_ref[...]
        hc_scr[...] = h0_ref[...]

    lam = lam_ref[...]
    half_coef = (-0.5 * LRU_C * LOG2E) * (jnp.maximum(-lam, 0.0)
                                         + jnp.log1p(jnp.exp(-jnp.abs(lam))))
    srow = lax.broadcasted_iota(jnp.int32, (SUBLANES, LRU_BLOCK), 0)

    sh1 = mod_ref[:, 0:D]
    sc1 = mod_ref[:, D:2 * D]
    u = (x_ref[...] * (1.0 + sc1) + sh1).astype(_BF16)

    def project(c0):
        return jnp.dot(u, w_ref[:, c0:c0 + PROJ_W], preferred_element_type=_F32)

    def project_rnn(k):
        c0 = k * PROJ_W
        xr = project(COL_XR * D + c0)
        conv_ref[:, c0:c0 + PROJ_W] = xr[tt - n_hist:tt, :]
        for m in range(PROJ_W // LRU_BLOCK):
            for s in range(SUBLANES):
                stage[k * (PROJ_W // LRU_BLOCK) + m, s * pitch:s * pitch + seg, :] = (
                    xr[s * seg:(s + 1) * seg, m * LRU_BLOCK:(m + 1) * LRU_BLOCK])
        gl_scr[:, c0:c0 + PROJ_W] = project(COL_GL * D + c0)
        ga_scr[:, c0:c0 + PROJ_W] = project(COL_GA * D + c0)

    z_rows = tt // Z_ROW_SPLIT if tt >= Z_ROW_SPLIT * Z_MIN_ROWS else tt

    def project_z(k):
        kc, kr = divmod(k, tt // z_rows)
        col, off = divmod(kc * PROJ_W, D)
        c0 = Z_SOURCE_COLS[col] * D + off
        rows = slice(kr * z_rows, (kr + 1) * z_rows)
        zk = jnp.dot(u[rows], w_ref[:, c0:c0 + PROJ_W], preferred_element_type=_F32)
        if Z_SOURCE_COLS[col] == COL_Q:
            zk = zk * Q_SCALE
        elif Z_SOURCE_COLS[col] == COL_GB:
            zk = 0.5 * jnp.tanh(zk) + 0.5
        z_ref[rows, kc * PROJ_W:(kc + 1) * PROJ_W] = zk.astype(_BF16)

    def step_regs(v):
        return [v[g * SUBLANES:(g + 1) * SUBLANES] for g in range(seg)]

    def lru_block(n):
        cols = slice(n * LRU_BLOCK, (n + 1) * LRU_BLOCK)
        xg = [stage[n, pl.ds(g, SUBLANES, stride=pitch), :] for g in range(seg)]
        xt = jnp.concatenate(xg, axis=0)

        pre = []
        for j in range(1, n_hist + 1):
            hrow = jnp.broadcast_to(hist_scr[n_hist - j:n_hist - j + 1, cols], (SUBLANES, LRU_BLOCK))
            pre.append(jnp.where(srow == 0, hrow, pltpu.roll(xg[seg - j], 1, 0)))

        def back(j):
            head = [pre[k - 1] for k in range(j, 0, -1)]
            return jnp.concatenate(head + [xt[:tt - j * SUBLANES]], axis=0)

        xc = cb_ref[:, cols] + back(n_hist) * cw_ref[0:1, cols]
        for j in range(1, n_hist):
            xc = xc + back(n_hist - j) * cw_ref[j:j + 1, cols]
        xc = xc + xt * cw_ref[n_hist:CONV_W, cols]

        yield
        gates = jnp.dot(xc.astype(_BF16), wg_ref[n], preferred_element_type=_F32) + bg_ref[n]
        coef = half_coef[:, cols]
        a = jnp.exp2(coef * jnp.tanh(gates[:, :LRU_BLOCK]) + coef)
        mult = jnp.exp2(0.5 * jnp.log2(1.0 - a * a))
        half_xc = 0.5 * xc
        ix = half_xc * jnp.tanh(gates[:, LRU_BLOCK:]) + half_xc
        av = step_regs(a)
        bv = step_regs(mult * ix)
        if reset_first:
            first = jnp.logical_and(srow == 0, i == 0)
            av[0] = jnp.where(first, 0.0, av[0])
            bv[0] = jnp.where(first, ix[:SUBLANES], bv[0])

        yield
        seg_a, seg_b = av[0], bv[0]
        for g in range(1, seg):
            seg_b = av[g] * seg_b + bv[g]
            seg_a = av[g] * seg_a
            if g == seg // 2:
                yield
        for d in (1, 2, 4):
            keep = srow >= d
            a_sh = jnp.where(keep, pltpu.roll(seg_a, d, 0), 1.0)
            b_sh = jnp.where(keep, pltpu.roll(seg_b, d, 0), 0.0)
            seg_b = seg_a * b_sh + seg_b
            seg_a = seg_a * a_sh
        yield
        h_in = jnp.broadcast_to(hc_scr[:, cols], (SUBLANES, LRU_BLOCK))
        h_end = seg_a * h_in + seg_b
        h = jnp.where(srow == 0, h_in, pltpu.roll(h_end, 1, 0))
        for g in range(seg):
            h = av[g] * h + bv[g]
            stage[n, pl.ds(g, SUBLANES, stride=pitch), :] = h
            if g == seg // 2:
                yield
        hc_scr[:, cols] = h[SUBLANES - 1:SUBLANES]
        hlast_ref[:, cols] = h[SUBLANES - 1:SUBLANES]

        yield
        for s in range(SUBLANES):
            rows = slice(s * seg, (s + 1) * seg)
            gl = gl_scr[rows, cols]
            hq = stage[n, s * pitch:s * pitch + seg, :] * (0.25 * gl)
            w = hq + hq * jnp.tanh(gl * (GELU_C1 + GELU_C2 * (gl * gl)))
            ya_ref[rows, cols] = (w + w * jnp.tanh(ga_scr[rows, cols])).astype(_BF16)
            if s == SUBLANES // 2 - 1:
                yield

    blocks_per_chunk = PROJ_W // LRU_BLOCK
    n_z = (D_Z // PROJ_W) * (tt // z_rows)
    n_slots = N_LRU_BLOCKS * LRU_STAGES
    slot = 0
    project_rnn(0)
    for n in range(N_LRU_BLOCKS):
        if n % blocks_per_chunk == 0 and n + blocks_per_chunk < N_LRU_BLOCKS:
            project_rnn(n // blocks_per_chunk + 1)
        for _ in lru_block(n):
            for k in range(slot * n_z // n_slots, (slot + 1) * n_z // n_slots):
                project_z(k)
            slot += 1
    assert slot == n_slots

    hist_scr[...] = conv_ref[...]


def _mix_in(x2, mod3, w_in, hist, h0, conv_w, conv_b, wg, bg, lam, bsz, t, tt, reset_first):
    nt = t // tt
    kern = functools.partial(_mix_in_kernel, tt=tt, reset_first=reset_first)
    return pl.pallas_call(
        kern,
        grid=(bsz, nt),
        in_specs=[
            pl.BlockSpec((tt, D), lambda b, i: (b * nt + i, 0)),
            pl.BlockSpec((None, 1, 6 * D), lambda b, i: (b, 0, 0)),
            _resident((D, D_IN)),
            pl.BlockSpec((None, CONV_W - 1, D), lambda b, i: (b, 0, 0)),
            pl.BlockSpec((None, 1, D), lambda b, i: (b, 0, 0)),
            _resident((CONV_W, D)),
            _resident((1, D)),
            _resident((N_LRU_BLOCKS, LRU_BLOCK, 2 * LRU_BLOCK)),
            _resident((N_LRU_BLOCKS, 1, 2 * LRU_BLOCK)),
            _resident((1, D)),
        ],
        out_specs=[
            pl.BlockSpec((tt, D_Z), lambda b, i: (b * nt + i, 0)),
            pl.BlockSpec((tt, D), lambda b, i: (b * nt + i, 0)),
            pl.BlockSpec((None, CONV_W - 1, D), lambda b, i: (b, 0, 0)),
            pl.BlockSpec((None, 1, D), lambda b, i: (b, 0, 0)),
        ],
        out_shape=[
            jax.ShapeDtypeStruct((bsz * t, D_Z), _BF16),
            jax.ShapeDtypeStruct((bsz * t, D), _BF16),
            jax.ShapeDtypeStruct((bsz, CONV_W - 1, D), _F32),
            jax.ShapeDtypeStruct((bsz, 1, D), _F32),
        ],
        scratch_shapes=[
            pltpu.VMEM((N_LRU_BLOCKS, SUBLANES * _seg_pitch(tt // SUBLANES), LRU_BLOCK), _F32),
            pltpu.VMEM((tt, D), _F32),
            pltpu.VMEM((tt, D), _F32),
            pltpu.VMEM((CONV_W - 1, D), _F32),
            pltpu.VMEM((1, D), _F32),
        ],
        compiler_params=pltpu.CompilerParams(
            dimension_semantics=("arbitrary", "arbitrary"), vmem_limit_bytes=VMEM_LIMIT),
        name="mix_in",
    )(x2, mod3, w_in, hist, h0, conv_w, conv_b, wg, bg, lam)


def _attn_out_kernel(q_ref, kp_ref, kc_ref, vp_ref, vc_ref, sgb_ref, row0_ref,
                     x_ref, ya_ref, mod_ref, wo_ref, wu_ref, bu_ref, wd_ref, bd_ref, ln_ref,
                     y_ref, bias_scr, sc_scr, yb_scr, *, tm, sb, band, nt):
    j = pl.program_id(0)
    i = jnp.minimum(j, pl.num_programs(0) - 2) % nt
    past = kp_ref.shape[0]
    win = past + sb
    nsub = tm // sb
    slot_w = j % 2
    slot_r = 1 - slot_w

    @pl.when(j == 0)
    def _first_step():
        yb_scr[...] = jnp.zeros_like(yb_scr)
        qq = lax.broadcasted_iota(jnp.int32, (sb, win), 0)
        kk = lax.broadcasted_iota(jnp.int32, (sb, win), 1)
        if band:
            dc = (kk // CHUNK) - (qq // CHUNK)
            ok = jnp.logical_and(dc >= 0, dc <= BAND_ROWS // CHUNK)
        else:
            ok = kk < past + CHUNK
        for h in range(N_HEADS):
            base = jnp.broadcast_to(row0_ref[h:h + 1, :], (sb, ROLL_W))
            toep = pltpu.roll(base, 0, 1, stride=1, stride_axis=0)[:, :win] * LOG2E
            bias_scr[h] = jnp.where(ok, toep, NEG_INF)

    start_pen = jnp.where(i == 0, NEG_INF, 0.0) if band else None

    def rows(prev_ref, cur_ref, so, cols):
        parts = []
        if so < past:
            parts.append(prev_ref[so:past, cols])
        parts.append(cur_ref[max(so - past, 0):so + sb, cols])
        return parts[0] if len(parts) == 1 else jnp.concatenate(parts, axis=0)

    def scores(n, s, h):
        cols = slice(h * HEAD_DIM, (h + 1) * HEAD_DIM)
        qh = q_ref[s * sb:(s + 1) * sb, cols]
        kh = rows(kp_ref, kc_ref, s * sb, cols)
        sc = lax.dot_general(qh, kh, (((1,), (1,)), ((), ())), preferred_element_type=_F32)
        sc = sc + bias_scr[h]
        before_start = past - s * sb
        if band and before_start > 0:
            sc = jnp.concatenate([sc[:, :before_start] + start_pen, sc[:, before_start:]], axis=1)
        sc_scr[n % SC_SLOTS] = sc

    def probs(n, s, h):
        sc = sc_scr[n % SC_SLOTS]
        return jnp.exp2(sc - jnp.max(sc, axis=-1, keepdims=True)).astype(_BF16)

    ones_cols = jnp.ones((win, HEAD_DIM), _BF16)

    def weighted_values(s, h, p):
        cols = slice(h * HEAD_DIM, (h + 1) * HEAD_DIM)
        vh = jnp.concatenate([rows(vp_ref, vc_ref, s * sb, cols), ones_cols], axis=1)
        ol = jnp.dot(p, vh, preferred_element_type=_F32)
        gate = sgb_ref[s * sb:(s + 1) * sb, cols].astype(_F32)
        scale = gate * (1.0 / ol[:, HEAD_DIM:])
        yb_scr[slot_w, s * sb:(s + 1) * sb, cols] = (ol[:, :HEAD_DIM] * scale).astype(_BF16)

    def attention_stages():
        pairs = [(s, h) for s in range(nsub) for h in range(N_HEADS)]
        for n in range(min(SC_AHEAD, len(pairs))):
            scores(n, *pairs[n])
        ready = probs(0, *pairs[0])
        for n, (s, h) in enumerate(pairs):
            yield
            if n + SC_AHEAD < len(pairs):
                scores(n + SC_AHEAD, *pairs[n + SC_AHEAD])
            current = ready
            if n + 1 < len(pairs):
                ready = probs(n + 1, *pairs[n + 1])
            weighted_values(s, h, current)

    g1 = mod_ref[:, 2 * D:3 * D]
    sh2 = mod_ref[:, 3 * D:4 * D]
    sc2 = mod_ref[:, 4 * D:5 * D]
    g2 = mod_ref[:, 5 * D:6 * D]

    def token_stages(rows):
        yb = yb_scr[slot_r, rows, :]
        merged = (ya_ref[rows, :].astype(_F32) + yb.astype(_F32)).astype(_BF16)
        yield
        o = jnp.dot(merged, wo_ref[...], preferred_element_type=_F32)
        yield
        x1 = _layer_norm(ALPHA * x_ref[rows, :] + (1.0 + g1) * o, ln_ref[0:1, :], ln_ref[1:2, :])
        u2 = (x1 * (1.0 + sc2) + sh2).astype(_BF16)
        yield
        f = bd_ref[...]
        for c in range(D_FF // FF_CHUNK):
            cols = slice(c * FF_CHUNK, (c + 1) * FF_CHUNK)
            hid = jnp.dot(u2, wu_ref[:, cols], preferred_element_type=_F32) + bu_ref[:, cols]
            hid = jnp.square(jnp.maximum(hid, 0.0)).astype(_BF16)
            yield
            f = f + jnp.dot(hid, wd_ref[cols, :], preferred_element_type=_F32)
            yield
        y_ref[rows, :] = _layer_norm(ALPHA * x1 + (1.0 + g2) * f, ln_ref[2:3, :], ln_ref[3:4, :])

    def staggered(gens):
        live = list(range(len(gens)))
        tick = 0
        while True:
            for k in list(live):
                if tick >= k and next(gens[k], _DONE) is _DONE:
                    live.remove(k)
            tick += 1
            if not live:
                return
            yield

    parts = TOKEN_GROUP_PARTS if tm >= sum(TOKEN_GROUP_PARTS) * TOKEN_MIN_PART_ROWS else (1,)
    unit = tm // sum(parts)
    bounds = [unit * sum(parts[:k]) for k in range(len(parts) + 1)]
    tokens = staggered([token_stages(slice(bounds[k], bounds[k + 1]))
                        for k in range(len(parts))])
    attention = attention_stages()
    n_att = nsub * N_HEADS + 1
    n_tok = len(parts) - 1 + 4 + 2 * (D_FF // FF_CHUNK)
    done_att = done_tok = 0
    while done_att < n_att or done_tok < n_tok:
        if done_tok == n_tok or (done_att < n_att and done_att * n_tok <= done_tok * n_att):
            next(attention, _DONE)
            done_att += 1
        else:
            next(tokens, _DONE)
            done_tok += 1
    assert next(attention, _DONE) is _DONE and next(tokens, _DONE) is _DONE


def _attn_out(z, kp_arr, vp_arr, sample_cache, row0, x2, ya, mod3, w_out, w_up, b_up, w_down,
              b_down, ln, bsz, t, tm, sb, past, band):
    nt = t // tm
    n_tiles = bsz * nt
    win = past + sb
    kern = functools.partial(_attn_out_kernel, tm=tm, sb=sb, band=band, nt=nt)

    def att_tile(j):
        return jnp.minimum(j, n_tiles - 1)

    def tok_tile(j):
        return jnp.maximum(j - 1, 0)

    def prev_rows(col):
        if sample_cache:
            return lambda j: (att_tile(j), 0)
        return lambda j: (jnp.where(att_tile(j) % nt == 0, att_tile(j), att_tile(j) - 1), col)

    return pl.pallas_call(
        kern,
        grid=(n_tiles + 1,),
        in_specs=[
            pl.BlockSpec((tm, D), lambda j: (att_tile(j), Z_Q)),
            pl.BlockSpec((past, D), prev_rows(Z_K)),
            pl.BlockSpec((tm, D), lambda j: (att_tile(j), Z_K)),
            pl.BlockSpec((past, D), prev_rows(Z_V)),
            pl.BlockSpec((tm, D), lambda j: (att_tile(j), Z_V)),
            pl.BlockSpec((tm, D), lambda j: (att_tile(j), Z_GB)),
            _resident((N_HEADS, ROLL_W)),
            pl.BlockSpec((tm, D), lambda j: (tok_tile(j), 0)),
            pl.BlockSpec((tm, D), lambda j: (tok_tile(j), 0)),
            pl.BlockSpec((None, 1, 6 * D), lambda j: (tok_tile(j) // nt, 0, 0)),
            _resident((D, D)),
            _resident((D, D_FF)),
            _resident((1, D_FF)),
            _resident((D_FF, D)),
            _resident((1, D)),
            _resident((4, D)),
        ],
        out_specs=pl.BlockSpec((tm, D), lambda j: (tok_tile(j), 0)),
        out_shape=jax.ShapeDtypeStruct((bsz * t, D), _F32),
        scratch_shapes=[
            pltpu.VMEM((N_HEADS, sb, win), _F32),
            pltpu.VMEM((SC_SLOTS, sb, win), _F32),
            pltpu.VMEM((2, tm, D), _BF16),
        ],
        compiler_params=pltpu.CompilerParams(
            dimension_semantics=("arbitrary",), vmem_limit_bytes=VMEM_LIMIT_ATTN_OUT),
        name="attn_out",
    )(z, kp_arr, z, vp_arr, z, z, row0, x2, ya, mod3, w_out, w_up, b_up, w_down, b_down, ln)


def _rel_bias_row0(table, past):
    far = jnp.broadcast_to(table[:, 2 * MAX_REL:], (N_HEADS, past - MAX_REL + 1))
    near = table[:, 2 * MAX_REL - 1::-1]
    tail = jnp.broadcast_to(table[:, 2 * MAX_REL:],
                            (N_HEADS, ROLL_W - (past + MAX_REL + 1)))
    return jnp.concatenate([far, near, tail], axis=1).astype(_F32)


def _heads(a, bsz, rows):
    return a.reshape(1, bsz, rows, N_HEADS, HEAD_DIM)


@jax.jit
def _forward(x_prompt, x_sample, c_prompt, c_sample, cache_k, cache_v, state_conv, state_lru,
             w_ada, b_ada, w_in, conv_w, conv_b, w_rg, b_rg, w_ig, b_ig, lru_lambda, rel_bias,
             w_out, ln1_g, ln1_b, w_up, b_up, w_down, b_down, ln2_g, ln2_b):
    bp, t, _ = x_prompt.shape
    bs, s, _ = x_sample.shape
    l = 0
    col_scale = jnp.where(jnp.arange(D_IN) >= COL_GA * D, 0.5, 1.0).astype(_F32)
    w_in_b = (w_in[l] * col_scale[None, :]).astype(_BF16)
    w_out_b = w_out[l].astype(_BF16)
    w_up_b = w_up[l].astype(_BF16)
    w_down_b = w_down[l].astype(_BF16)
    wg = (0.5 * jnp.concatenate([w_rg[l], w_ig[l]], axis=-1)).astype(_BF16)
    bg = 0.5 * jnp.concatenate([b_rg[l], b_ig[l]], axis=-1)[:, None, :]
    lam = lru_lambda[l][None, :]
    cb = conv_b[l][None, :]
    ln = jnp.stack([ln1_g[l], ln1_b[l], ln2_g[l], ln2_b[l]])
    bu = b_up[l][None, :]
    bd = b_down[l][None, :]

    mod = _ada(jnp.concatenate([c_prompt, c_sample], axis=0), w_ada[l].astype(_BF16),
               b_ada[l][None, :])
    mod_p = mod[:bp, None, :]
    mod_s = mod[bp:, None, :]

    tm = 512
    xp2 = x_prompt.reshape(bp * t, D)
    z, ya, conv_p, h_p = _mix_in(xp2, mod_p, w_in_b, jnp.zeros((bp, CONV_W - 1, D), _F32),
                                 jnp.zeros((bp, 1, D), _F32), conv_w[l], cb, wg, bg, lam,
                                 bp, t, tm, True)
    y_p = _attn_out(z, z, z, False, _rel_bias_row0(rel_bias[l], BAND_ROWS), xp2, ya, mod_p,
                    w_out_b, w_up_b, bu, w_down_b, bd, ln, bp, t, tm, 2 * CHUNK, BAND_ROWS, True)
    z3 = z.reshape(bp, t, D_Z)
    rows = min(BAND_ROWS, t)
    k_p = z3[:, t - rows:, Z_K * D:(Z_K + 1) * D].astype(_F32)
    v_p = z3[:, t - rows:, Z_V * D:(Z_V + 1) * D].astype(_F32)

    xs2 = x_sample.reshape(bs * s, D)
    zs, ya_s, conv_s, h_s = _mix_in(xs2, mod_s, w_in_b, state_conv[l], state_lru[l][:, None, :],
                                    conv_w[l], cb, wg, bg, lam, bs, s, s, False)
    sbs = 2 * CHUNK
    zs3 = zs.reshape(bs, s, D_Z)

    def pad_rows(a):
        a3 = a.reshape(bs, s, a.shape[-1])
        return jnp.pad(a3, ((0, 0), (0, sbs - s), (0, 0))).reshape(bs * sbs, a.shape[-1])

    n_past = cache_k.shape[2]
    ck = cache_k[l].reshape(bs * n_past, D).astype(_BF16)
    cv = cache_v[l].reshape(bs * n_past, D).astype(_BF16)
    y_s = _attn_out(pad_rows(zs), ck, cv, True, _rel_bias_row0(rel_bias[l], n_past),
                    pad_rows(xs2), pad_rows(ya_s), mod_s, w_out_b, w_up_b, bu, w_down_b, bd, ln,
                    bs, sbs, sbs, sbs, n_past, False)
    y_s = y_s.reshape(bs, sbs, D)[:, :s]
    k_s = zs3[:, :, Z_K * D:(Z_K + 1) * D].astype(_F32)
    v_s = zs3[:, :, Z_V * D:(Z_V + 1) * D].astype(_F32)

    return (y_p.reshape(bp, t, D), y_s.reshape(bs, s, D),
            _heads(k_p, bp, rows), _heads(v_p, bp, rows), conv_p[None], h_p.reshape(1, bp, D),
            _heads(k_s, bs, s), _heads(v_s, bs, s), conv_s[None], h_s.reshape(1, bs, D))


def kernel(x_prompt, x_sample, c_prompt, c_sample, cache_k, cache_v, state_conv, state_lru, w_ada, b_ada, w_in, conv_w, conv_b, w_rg, b_rg, w_ig, b_ig, lru_lambda, rel_bias, w_out, ln1_g, ln1_b, w_up, b_up, w_down, b_down, ln2_g, ln2_b):
    return _forward(x_prompt, x_sample, c_prompt, c_sample, cache_k, cache_v, state_conv,
                    state_lru, w_ada, b_ada, w_in, conv_w, conv_b, w_rg, b_rg, w_ig, b_ig,
                    lru_lambda, rel_bias, w_out, ln1_g, ln1_b, w_up, b_up, w_down, b_down,
                    ln2_g, ln2_b)
```

```python
import functools
import math

import jax
import jax.numpy as jnp
from jax import lax
from jax.experimental import pallas as pl
from jax.experimental.pallas import tpu as pltpu

D = 1024
CHUNK = 64
BAND_ROWS = 8 * CHUNK
N_HEADS = 8
HEAD_DIM = D // N_HEADS
N_LRU_BLOCKS = 8
LRU_BLOCK = D // N_LRU_BLOCKS
CONV_W = 4
LRU_C = 8.0
MAX_REL = 128
ATT_SCALE = HEAD_DIM ** -0.5
NEG_INF = -1e30
D_FF = 4 * D
D_IN = 7 * D
DEPTH = 1
ALPHA = (2 * DEPTH) ** 0.25
LN_EPS = 1e-5

LOG2E = math.log2(math.e)
Q_SCALE = ATT_SCALE * LOG2E
ROLL_W = 768
SC_AHEAD = 2
SC_SLOTS = SC_AHEAD + 1
GELU_C1 = math.sqrt(2.0 / math.pi)
GELU_C2 = GELU_C1 * 0.044715

SUBLANES = 8
VMEM_LIMIT = 56 * 1024 * 1024
VMEM_LIMIT_ATTN_OUT = 62 * 1024 * 1024

COL_XR, COL_GL, COL_Q, COL_K, COL_V, COL_GA, COL_GB = range(7)
Z_SOURCE_COLS = (COL_Q, COL_K, COL_V, COL_GB)
Z_Q, Z_K, Z_V, Z_GB = range(4)
D_Z = len(Z_SOURCE_COLS) * D
PROJ_W = 256
Z_ROW_SPLIT = 2
Z_MIN_ROWS = 256
LRU_STAGES = 7
TOKEN_GROUP_PARTS = (1, 1)
TOKEN_MIN_PART_ROWS = 256
FF_CHUNK = 512

_F32 = jnp.float32
_BF16 = jnp.bfloat16
_DONE = object()


def _resident(shape):
    n = len(shape)
    return pl.BlockSpec(shape, lambda *_: (0,) * n, pipeline_mode=pl.Buffered(1))


def _sigmoid(v):
    return 0.5 * jnp.tanh(0.5 * v) + 0.5


def _layer_norm(v, g, b):
    mu = jnp.mean(v, axis=-1, keepdims=True)
    vc = v - mu
    var = jnp.mean(vc * vc, axis=-1, keepdims=True)
    return vc * lax.rsqrt(var + LN_EPS) * g + b


def _ada_kernel(c_ref, w_ref, b_ref, o_ref):
    c = c_ref[...]
    s = (c * _sigmoid(c)).astype(_BF16)
    o_ref[...] = jnp.dot(s, w_ref[...], preferred_element_type=_F32) + b_ref[...]


def _ada(c, w_ada, b_ada):
    n = c.shape[0]
    return pl.pallas_call(
        _ada_kernel,
        out_shape=jax.ShapeDtypeStruct((n, 6 * D), _F32),
        compiler_params=pltpu.CompilerParams(vmem_limit_bytes=VMEM_LIMIT),
        name="ada",
    )(c, w_ada, b_ada)


def _seg_pitch(seg):
    return seg if (seg // SUBLANES) % 2 == 1 else seg + SUBLANES


def _mix_in_kernel(x_ref, mod_ref, w_ref, hist_ref, h0_ref, cw_ref, cb_ref, wg_ref, bg_ref,
                   lam_ref, z_ref, ya_ref, conv_ref, hlast_ref, stage, gl_scr, ga_scr, hist_scr,
                   hc_scr, *, tt, reset_first):
    i = pl.program_id(1)
    seg = tt // SUBLANES
    pitch = _seg_pitch(seg)
    n_hist = CONV_W - 1

    @pl.when(i == 0)
    def _():
        hist_scr[...] = hist_ref[...]
        hc_scr[...] = h0_ref[...]

    lam = lam_ref[...]
    half_coef = (-0.5 * LRU_C * LOG2E) * (jnp.maximum(-lam, 0.0)
                                         + jnp.log1p(jnp.exp(-jnp.abs(lam))))
    srow = lax.broadcasted_iota(jnp.int32, (SUBLANES, LRU_BLOCK), 0)

    sh1 = mod_ref[:, 0:D]
    sc1 = mod_ref[:, D:2 * D]
    u = (x_ref[...] * (1.0 + sc1) + sh1).astype(_BF16)

    def project(c0):
        return jnp.dot(u, w_ref[:, c0:c0 + PROJ_W], preferred_element_type=_F32)

    def project_rnn(k):
        c0 = k * PROJ_W
        xr = project(COL_XR * D + c0)
        conv_ref[:, c0:c0 + PROJ_W] = xr[tt - n_hist:tt, :]
        for m in range(PROJ_W // LRU_BLOCK):
            for s in range(SUBLANES):
                stage[k * (PROJ_W // LRU_BLOCK) + m, s * pitch:s * pitch + seg, :] = (
                    xr[s * seg:(s + 1) * seg, m * LRU_BLOCK:(m + 1) * LRU_BLOCK])
        gl_scr[:, c0:c0 + PROJ_W] = project(COL_GL * D + c0)
        ga_scr[:, c0:c0 + PROJ_W] = project(COL_GA * D + c0)

    z_rows = tt // Z_ROW_SPLIT if tt >= Z_ROW_SPLIT * Z_MIN_ROWS else tt

    def project_z(k):
        kc, kr = divmod(k, tt // z_rows)
        col, off = divmod(kc * PROJ_W, D)
        c0 = Z_SOURCE_COLS[col] * D + off
        rows = slice(kr * z_rows, (kr + 1) * z_rows)
        zk = jnp.dot(u[rows], w_ref[:, c0:c0 + PROJ_W], preferred_element_type=_F32)
        if Z_SOURCE_COLS[col] == COL_Q:
            zk = zk * Q_SCALE
        elif Z_SOURCE_COLS[col] == COL_GB:
            zk = 0.5 * jnp.tanh(zk) + 0.5
        z_ref[rows, kc * PROJ_W:(kc + 1) * PROJ_W] = zk.astype(_BF16)

    def step_regs(v):
        return [v[g * SUBLANES:(g + 1) * SUBLANES] for g in range(seg)]

    def lru_block(n):
        cols = slice(n * LRU_BLOCK, (n + 1) * LRU_BLOCK)
        xg = [stage[n, pl.ds(g, SUBLANES, stride=pitch), :] for g in range(seg)]
        xt = jnp.concatenate(xg, axis=0)

        pre = []
        for j in range(1, n_hist + 1):
            hrow = jnp.broadcast_to(hist_scr[n_hist - j:n_hist - j + 1, cols], (SUBLANES, LRU_BLOCK))
            pre.append(jnp.where(srow == 0, hrow, pltpu.roll(xg[seg - j], 1, 0)))

        def back(j):
            head = [pre[k - 1] for k in range(j, 0, -1)]
            return jnp.concatenate(head + [xt[:tt - j * SUBLANES]], axis=0)

        xc = cb_ref[:, cols] + back(n_hist) * cw_ref[0:1, cols]
        for j in range(1, n_hist):
            xc = xc + back(n_hist - j) * cw_ref[j:j + 1, cols]
        xc = xc + xt * cw_ref[n_hist:CONV_W, cols]

        yield
        gates = jnp.dot(xc.astype(_BF16), wg_ref[n], preferred_element_type=_F32) + bg_ref[n]
        coef = half_coef[:, cols]
        a = jnp.exp2(coef * jnp.tanh(gates[:, :LRU_BLOCK]) + coef)
        mult = jnp.exp2(0.5 * jnp.log2(1.0 - a * a))
        half_xc = 0.5 * xc
        ix = half_xc * jnp.tanh(gates[:, LRU_BLOCK:]) + half_xc
        av = step_regs(a)
        bv = step_regs(mult * ix)
        if reset_first:
            first = jnp.logical_and(srow == 0, i == 0)
            av[0] = jnp.where(first, 0.0, av[0])
            bv[0] = jnp.where(first, ix[:SUBLANES], bv[0])

        yield
        seg_a, seg_b = av[0], bv[0]
        for g in range(1, seg):
            seg_b = av[g] * seg_b + bv[g]
            seg_a = av[g] * seg_a
            if g == seg // 2:
                yield
        for d in (1, 2, 4):
            keep = srow >= d
            a_sh = jnp.where(keep, pltpu.roll(seg_a, d, 0), 1.0)
            b_sh = jnp.where(keep, pltpu.roll(seg_b, d, 0), 0.0)
            seg_b = seg_a * b_sh + seg_b
            seg_a = seg_a * a_sh
        yield
        h_in = jnp.broadcast_to(hc_scr[:, cols], (SUBLANES, LRU_BLOCK))
        h_end = seg_a * h_in + seg_b
        h = jnp.where(srow == 0, h_in, pltpu.roll(h_end, 1, 0))
        for g in range(seg):
            h = av[g] * h + bv[g]
            stage[n, pl.ds(g, SUBLANES, stride=pitch), :] = h
            if g == seg // 2:
                yield
        hc_scr[:, cols] = h[SUBLANES - 1:SUBLANES]
        hlast_ref[:, cols] = h[SUBLANES - 1:SUBLANES]

        yield
        for s in range(SUBLANES):
            rows = slice(s * seg, (s + 1) * seg)
            gl = gl_scr[rows, cols]
            hq = stage[n, s * pitch:s * pitch + seg, :] * (0.25 * gl)
            w = hq + hq * jnp.tanh(gl * (GELU_C1 + GELU_C2 * (gl * gl)))
            ya_ref[rows, cols] = (w + w * jnp.tanh(ga_scr[rows, cols])).astype(_BF16)
            if s == SUBLANES // 2 - 1:
                yield

    blocks_per_chunk = PROJ_W // LRU_BLOCK
    n_z = (D_Z // PROJ_W) * (tt // z_rows)
    n_slots = N_LRU_BLOCKS * LRU_STAGES
    slot = 0
    project_rnn(0)
    for n in range(N_LRU_BLOCKS):
        if n % blocks_per_chunk == 0 and n + blocks_per_chunk < N_LRU_BLOCKS:
            project_rnn(n // blocks_per_chunk + 1)
        for _ in lru_block(n):
            for k in range(slot * n_z // n_slots, (slot + 1) * n_z // n_slots):
                project_z(k)
            slot += 1
    assert slot == n_slots

    hist_scr[...] = conv_ref[...]


def _mix_in(x2, mod3, w_in, hist, h0, conv_w, conv_b, wg, bg, lam, bsz, t, tt, reset_first):
    nt = t // tt
    kern = functools.partial(_mix_in_kernel, tt=tt, reset_first=reset_first)
    return pl.pallas_call(
        kern,
        grid=(bsz, nt),
        in_specs=[
            pl.BlockSpec((tt, D), lambda b, i: (b * nt + i, 0)),
            pl.BlockSpec((None, 1, 6 * D), lambda b, i: (b, 0, 0)),
            _resident((D, D_IN)),
            pl.BlockSpec((None, CONV_W - 1, D), lambda b, i: (b, 0, 0)),
            pl.BlockSpec((None, 1, D), lambda b, i: (b, 0, 0)),
            _resident((CONV_W, D)),
            _resident((1, D)),
            _resident((N_LRU_BLOCKS, LRU_BLOCK, 2 * LRU_BLOCK)),
            _resident((N_LRU_BLOCKS, 1, 2 * LRU_BLOCK)),
            _resident((1, D)),
        ],
        out_specs=[
            pl.BlockSpec((tt, D_Z), lambda b, i: (b * nt + i, 0)),
            pl.BlockSpec((tt, D), lambda b, i: (b * nt + i, 0)),
            pl.BlockSpec((None, CONV_W - 1, D), lambda b, i: (b, 0, 0)),
            pl.BlockSpec((None, 1, D), lambda b, i: (b, 0, 0)),
        ],
        out_shape=[
            jax.ShapeDtypeStruct((bsz * t, D_Z), _BF16),
            jax.ShapeDtypeStruct((bsz * t, D), _BF16),
            jax.ShapeDtypeStruct((bsz, CONV_W - 1, D), _F32),
            jax.ShapeDtypeStruct((bsz, 1, D), _F32),
        ],
        scratch_shapes=[
            pltpu.VMEM((N_LRU_BLOCKS, SUBLANES * _seg_pitch(tt // SUBLANES), LRU_BLOCK), _F32),
            pltpu.VMEM((tt, D), _F32),
            pltpu.VMEM((tt, D), _F32),
            pltpu.VMEM((CONV_W - 1, D), _F32),
            pltpu.VMEM((1, D), _F32),
        ],
        compiler_params=pltpu.CompilerParams(
            dimension_semantics=("arbitrary", "arbitrary"), vmem_limit_bytes=VMEM_LIMIT),
        name="mix_in",
    )(x2, mod3, w_in, hist, h0, conv_w, conv_b, wg, bg, lam)


def _attn_out_kernel(q_ref, kp_ref, kc_ref, vp_ref, vc_ref, sgb_ref, row0_ref,
                     x_ref, ya_ref, mod_ref, wo_ref, wu_ref, bu_ref, wd_ref, bd_ref, ln_ref,
                     y_ref, bias_scr, sc_scr, yb_scr, *, tm, sb, band, nt):
    j = pl.program_id(0)
    i = jnp.minimum(j, pl.num_programs(0) - 2) % nt
    past = kp_ref.shape[0]
    win = past + sb
    nsub = tm // sb
    slot_w = j % 2
    slot_r = 1 - slot_w

    @pl.when(j == 0)
    def _first_step():
        yb_scr[...] = jnp.zeros_like(yb_scr)
        qq = lax.broadcasted_iota(jnp.int32, (sb, win), 0)
        kk = lax.broadcasted_iota(jnp.int32, (sb, win), 1)
        if band:
            dc = (kk // CHUNK) - (qq // CHUNK)
            ok = jnp.logical_and(dc >= 0, dc <= BAND_ROWS // CHUNK)
        else:
            ok = kk < past + CHUNK
        for h in range(N_HEADS):
            base = jnp.broadcast_to(row0_ref[h:h + 1, :], (sb, ROLL_W))
            toep = pltpu.roll(base, 0, 1, stride=1, stride_axis=0)[:, :win] * LOG2E
            bias_scr[h] = jnp.where(ok, toep, NEG_INF)
        bias_scr[N_HEADS] = jnp.full((sb, win), NEG_INF, _F32)

    def rows(prev_ref, cur_ref, so, cols):
        parts = []
        if so < past:
            parts.append(prev_ref[so:past, cols])
        parts.append(cur_ref[max(so - past, 0):so + sb, cols])
        return parts[0] if len(parts) == 1 else jnp.concatenate(parts, axis=0)

    def scores(n, s, h):
        cols = slice(h * HEAD_DIM, (h + 1) * HEAD_DIM)
        qh = q_ref[s * sb:(s + 1) * sb, cols]
        kh = rows(kp_ref, kc_ref, s * sb, cols)
        sc = lax.dot_general(qh, kh, (((1,), (1,)), ((), ())), preferred_element_type=_F32)
        before_start = past - s * sb if band else 0
        biased = []
        for c0 in range(0, win, HEAD_DIM):
            lanes = slice(c0, c0 + HEAD_DIM)
            slot = jnp.where(i == 0, N_HEADS, h) if c0 + HEAD_DIM <= before_start else h
            biased.append(sc[:, lanes] + bias_scr[slot, :, lanes])
        sc_scr[n % SC_SLOTS] = jnp.concatenate(biased, axis=1)

    def probs(n, s, h):
        sc = sc_scr[n % SC_SLOTS]
        return jnp.exp2(sc - jnp.max(sc, axis=-1, keepdims=True)).astype(_BF16)

    ones_cols = jnp.ones((win, HEAD_DIM), _BF16)

    def weighted_values(s, h, p):
        cols = slice(h * HEAD_DIM, (h + 1) * HEAD_DIM)
        vh = jnp.concatenate([rows(vp_ref, vc_ref, s * sb, cols), ones_cols], axis=1)
        ol = jnp.dot(p, vh, preferred_element_type=_F32)
        gate = sgb_ref[s * sb:(s + 1) * sb, cols].astype(_F32)
        scale = gate * (1.0 / ol[:, HEAD_DIM:])
        yb_scr[slot_w, s * sb:(s + 1) * sb, cols] = (ol[:, :HEAD_DIM] * scale).astype(_BF16)

    def attention_stages():
        pairs = [(s, h) for s in range(nsub) for h in range(N_HEADS)]
        for n in range(min(SC_AHEAD, len(pairs))):
            scores(n, *pairs[n])
        ready = probs(0, *pairs[0])
        for n, (s, h) in enumerate(pairs):
            yield
            if n + SC_AHEAD < len(pairs):
                scores(n + SC_AHEAD, *pairs[n + SC_AHEAD])
            current = ready
            if n + 1 < len(pairs):
                ready = probs(n + 1, *pairs[n + 1])
            weighted_values(s, h, current)

    g1 = mod_ref[:, 2 * D:3 * D]
    sh2 = mod_ref[:, 3 * D:4 * D]
    sc2 = mod_ref[:, 4 * D:5 * D]
    g2 = mod_ref[:, 5 * D:6 * D]

    def token_stages(rows):
        yb = yb_scr[slot_r, rows, :]
        merged = (ya_ref[rows, :].astype(_F32) + yb.astype(_F32)).astype(_BF16)
        yield
        o = jnp.dot(merged, wo_ref[...], preferred_element_type=_F32)
        yield
        x1 = _layer_norm(ALPHA * x_ref[rows, :] + (1.0 + g1) * o, ln_ref[0:1, :], ln_ref[1:2, :])
        u2 = (x1 * (1.0 + sc2) + sh2).astype(_BF16)
        yield
        f = bd_ref[...]
        for c in range(D_FF // FF_CHUNK):
            cols = slice(c * FF_CHUNK, (c + 1) * FF_CHUNK)
            hid = jnp.dot(u2, wu_ref[:, cols], preferred_element_type=_F32) + bu_ref[:, cols]
            hid = jnp.square(jnp.maximum(hid, 0.0)).astype(_BF16)
            yield
            f = f + jnp.dot(hid, wd_ref[cols, :], preferred_element_type=_F32)
            yield
        y_ref[rows, :] = _layer_norm(ALPHA * x1 + (1.0 + g2) * f, ln_ref[2:3, :], ln_ref[3:4, :])

    def staggered(gens):
        live = list(range(len(gens)))
        tick = 0
        while True:
            for k in list(live):
                if tick >= k and next(gens[k], _DONE) is _DONE:
                    live.remove(k)
            tick += 1
            if not live:
                return
            yield

    parts = TOKEN_GROUP_PARTS if tm >= sum(TOKEN_GROUP_PARTS) * TOKEN_MIN_PART_ROWS else (1,)
    unit = tm // sum(parts)
    bounds = [unit * sum(parts[:k]) for k in range(len(parts) + 1)]
    tokens = staggered([token_stages(slice(bounds[k], bounds[k + 1]))
                        for k in range(len(parts))])
    attention = attention_stages()
    n_att = nsub * N_HEADS + 1
    n_tok = len(parts) - 1 + 4 + 2 * (D_FF // FF_CHUNK)
    done_att = done_tok = 0
    while done_att < n_att or done_tok < n_tok:
        if done_tok == n_tok or (done_att < n_att and done_att * n_tok <= done_tok * n_att):
            next(attention, _DONE)
            done_att += 1
        else:
            next(tokens, _DONE)
            done_tok += 1
    assert next(attention, _DONE) is _DONE and next(tokens, _DONE) is _DONE


def _attn_out(z, kp_arr, vp_arr, sample_cache, row0, x2, ya, mod3, w_out, w_up, b_up, w_down,
              b_down, ln, bsz, t, tm, sb, past, band):
    nt = t // tm
    n_tiles = bsz * nt
    win = past + sb
    kern = functools.partial(_attn_out_kernel, tm=tm, sb=sb, band=band, nt=nt)

    def att_tile(j):
        return jnp.minimum(j, n_tiles - 1)

    def tok_tile(j):
        return jnp.maximum(j - 1, 0)

    def prev_rows(col):
        if sample_cache:
            return lambda j: (att_tile(j), 0)
        return lambda j: (jnp.where(att_tile(j) % nt == 0, att_tile(j), att_tile(j) - 1), col)

    return pl.pallas_call(
        kern,
        grid=(n_tiles + 1,),
        in_specs=[
            pl.BlockSpec((tm, D), lambda j: (att_tile(j), Z_Q)),
            pl.BlockSpec((past, D), prev_rows(Z_K)),
            pl.BlockSpec((tm, D), lambda j: (att_tile(j), Z_K)),
            pl.BlockSpec((past, D), prev_rows(Z_V)),
            pl.BlockSpec((tm, D), lambda j: (att_tile(j), Z_V)),
            pl.BlockSpec((tm, D), lambda j: (att_tile(j), Z_GB)),
            _resident((N_HEADS, ROLL_W)),
            pl.BlockSpec((tm, D), lambda j: (tok_tile(j), 0)),
            pl.BlockSpec((tm, D), lambda j: (tok_tile(j), 0)),
            pl.BlockSpec((None, 1, 6 * D), lambda j: (tok_tile(j) // nt, 0, 0)),
            _resident((D, D)),
            _resident((D, D_FF)),
            _resident((1, D_FF)),
            _resident((D_FF, D)),
            _resident((1, D)),
            _resident((4, D)),
        ],
        out_specs=pl.BlockSpec((tm, D), lambda j: (tok_tile(j), 0)),
        out_shape=jax.ShapeDtypeStruct((bsz * t, D), _F32),
        scratch_shapes=[
            pltpu.VMEM((N_HEADS + 1, sb, win), _F32),
            pltpu.VMEM((SC_SLOTS, sb, win), _F32),
            pltpu.VMEM((2, tm, D), _BF16),
        ],
        compiler_params=pltpu.CompilerParams(
            dimension_semantics=("arbitrary",), vmem_limit_bytes=VMEM_LIMIT_ATTN_OUT),
        name="attn_out",
    )(z, kp_arr, z, vp_arr, z, z, row0, x2, ya, mod3, w_out, w_up, b_up, w_down, b_down, ln)


def _rel_bias_row0(table, past):
    far = jnp.broadcast_to(table[:, 2 * MAX_REL:], (N_HEADS, past - MAX_REL + 1))
    near = table[:, 2 * MAX_REL - 1::-1]
    tail = jnp.broadcast_to(table[:, 2 * MAX_REL:],
                            (N_HEADS, ROLL_W - (past + MAX_REL + 1)))
    return jnp.concatenate([far, near, tail], axis=1).astype(_F32)


def _heads(a, bsz, rows):
    return a.reshape(1, bsz, rows, N_HEADS, HEAD_DIM)


@jax.jit
def _forward(x_prompt, x_sample, c_prompt, c_sample, cache_k, cache_v, state_conv, state_lru,
             w_ada, b_ada, w_in, conv_w, conv_b, w_rg, b_rg, w_ig, b_ig, lru_lambda, rel_bias,
             w_out, ln1_g, ln1_b, w_up, b_up, w_down, b_down, ln2_g, ln2_b):
    bp, t, _ = x_prompt.shape
    bs, s, _ = x_sample.shape
    l = 0
    col_scale = jnp.where(jnp.arange(D_IN) >= COL_GA * D, 0.5, 1.0).astype(_F32)
    w_in_b = (w_in[l] * col_scale[None, :]).astype(_BF16)
    w_out_b = w_out[l].astype(_BF16)
    w_up_b = w_up[l].astype(_BF16)
    w_down_b = w_down[l].astype(_BF16)
    wg = (0.5 * jnp.concatenate([w_rg[l], w_ig[l]], axis=-1)).astype(_BF16)
    bg = 0.5 * jnp.concatenate([b_rg[l], b_ig[l]], axis=-1)[:, None, :]
    lam = lru_lambda[l][None, :]
    cb = conv_b[l][None, :]
    ln = jnp.stack([ln1_g[l], ln1_b[l], ln2_g[l], ln2_b[l]])
    bu = b_up[l][None, :]
    bd = b_down[l][None, :]

    mod = _ada(jnp.concatenate([c_prompt, c_sample], axis=0), w_ada[l].astype(_BF16),
               b_ada[l][None, :])
    mod_p = mod[:bp, None, :]
    mod_s = mod[bp:, None, :]

    tm = 512
    xp2 = x_prompt.reshape(bp * t, D)
    z, ya, conv_p, h_p = _mix_in(xp2, mod_p, w_in_b, jnp.zeros((bp, CONV_W - 1, D), _F32),
                                 jnp.zeros((bp, 1, D), _F32), conv_w[l], cb, wg, bg, lam,
                                 bp, t, tm, True)
    y_p = _attn_out(z, z, z, False, _rel_bias_row0(rel_bias[l], BAND_ROWS), xp2, ya, mod_p,
                    w_out_b, w_up_b, bu, w_down_b, bd, ln, bp, t, tm, 2 * CHUNK, BAND_ROWS, True)
    z3 = z.reshape(bp, t, D_Z)
    rows = min(BAND_ROWS, t)
    k_p = z3[:, t - rows:, Z_K * D:(Z_K + 1) * D].astype(_F32)
    v_p = z3[:, t - rows:, Z_V * D:(Z_V + 1) * D].astype(_F32)

    xs2 = x_sample.reshape(bs * s, D)
    zs, ya_s, conv_s, h_s = _mix_in(xs2, mod_s, w_in_b, state_conv[l], state_lru[l][:, None, :],
                                    conv_w[l], cb, wg, bg, lam, bs, s, s, False)
    sbs = 2 * CHUNK
    zs3 = zs.reshape(bs, s, D_Z)

    def pad_rows(a):
        a3 = a.reshape(bs, s, a.shape[-1])
        return jnp.pad(a3, ((0, 0), (0, sbs - s), (0, 0))).reshape(bs * sbs, a.shape[-1])

    n_past = cache_k.shape[2]
    ck = cache_k[l].reshape(bs * n_past, D).astype(_BF16)
    cv = cache_v[l].reshape(bs * n_past, D).astype(_BF16)
    y_s = _attn_out(pad_rows(zs), ck, cv, True, _rel_bias_row0(rel_bias[l], n_past),
                    pad_rows(xs2), pad_rows(ya_s), mod_s, w_out_b, w_up_b, bu, w_down_b, bd, ln,
                    bs, sbs, sbs, sbs, n_past, False)
    y_s = y_s.reshape(bs, sbs, D)[:, :s]
    k_s = zs3[:, :, Z_K * D:(Z_K + 1) * D].astype(_F32)
    v_s = zs3[:, :, Z_V * D:(Z_V + 1) * D].astype(_F32)

    return (y_p.reshape(bp, t, D), y_s.reshape(bs, s, D),
            _heads(k_p, bp, rows), _heads(v_p, bp, rows), conv_p[None], h_p.reshape(1, bp, D),
            _heads(k_s, bs, s), _heads(v_s, bs, s), conv_s[None], h_s.reshape(1, bs, D))


def kernel(x_prompt, x_sample, c_prompt, c_sample, cache_k, cache_v, state_conv, state_lru, w_ada, b_ada, w_in, conv_w, conv_b, w_rg, b_rg, w_ig, b_ig, lru_lambda, rel_bias, w_out, ln1_g, ln1_b, w_up, b_up, w_down, b_down, ln2_g, ln2_b):
    return _forward(x_prompt, x_sample, c_prompt, c_sample, cache_k, cache_v, state_conv,
                    state_lru, w_ada, b_ada, w_in, conv_w, conv_b, w_rg, b_rg, w_ig, b_ig,
                    lru_lambda, rel_bias, w_out, ln1_g, ln1_b, w_up, b_up, w_down, b_down,
                    ln2_g, ln2_b)
```

```python
import functools
import math

import jax
import jax.numpy as jnp
from jax import lax
from jax.experimental import pallas as pl
from jax.experimental.pallas import tpu as pltpu

D = 1024
CHUNK = 64
BAND_ROWS = 8 * CHUNK
N_HEADS = 8
HEAD_DIM = D // N_HEADS
N_LRU_BLOCKS = 8
LRU_BLOCK = D // N_LRU_BLOCKS
CONV_W = 4
LRU_C = 8.0
MAX_REL = 128
ATT_SCALE = HEAD_DIM ** -0.5
NEG_INF = -1e30
D_FF = 4 * D
D_IN = 7 * D
DEPTH = 1
ALPHA = (2 * DEPTH) ** 0.25
LN_EPS = 1e-5

LOG2E = math.log2(math.e)
Q_SCALE = ATT_SCALE * LOG2E
ROLL_W = 768
SC_AHEAD = 4
SC_SLOTS = SC_AHEAD + 1
GELU_C1 = math.sqrt(2.0 / math.pi)
GELU_C2 = GELU_C1 * 0.044715

SUBLANES = 8
VMEM_LIMIT = 56 * 1024 * 1024
VMEM_LIMIT_ATTN_OUT = 62 * 1024 * 1024

COL_XR, COL_GL, COL_Q, COL_K, COL_V, COL_GA, COL_GB = range(7)
Z_SOURCE_COLS = (COL_Q, COL_K, COL_V, COL_GB)
Z_Q, Z_K, Z_V, Z_GB = range(4)
D_Z = len(Z_SOURCE_COLS) * D
PROJ_W = 256
Z_ROW_SPLIT = 2
Z_MIN_ROWS = 256
LRU_STAGES = 7
TOKEN_GROUP_PARTS = (1, 1)
TOKEN_MIN_PART_ROWS = 256
FF_CHUNK = 512

_F32 = jnp.float32
_BF16 = jnp.bfloat16
_DONE = object()


def _resident(shape):
    n = len(shape)
    return pl.BlockSpec(shape, lambda *_: (0,) * n, pipeline_mode=pl.Buffered(1))


def _sigmoid(v):
    return 0.5 * jnp.tanh(0.5 * v) + 0.5


def _layer_norm(v, g, b):
    mu = jnp.mean(v, axis=-1, keepdims=True)
    vc = v - mu
    var = jnp.mean(vc * vc, axis=-1, keepdims=True)
    return vc * lax.rsqrt(var + LN_EPS) * g + b


def _ada_kernel(c_ref, w_ref, b_ref, o_ref):
    c = c_ref[...]
    s = (c * _sigmoid(c)).astype(_BF16)
    o_ref[...] = jnp.dot(s, w_ref[...], preferred_element_type=_F32) + b_ref[...]


def _ada(c, w_ada, b_ada):
    n = c.shape[0]
    return pl.pallas_call(
        _ada_kernel,
        out_shape=jax.ShapeDtypeStruct((n, 6 * D), _F32),
        compiler_params=pltpu.CompilerParams(vmem_limit_bytes=VMEM_LIMIT),
        name="ada",
    )(c, w_ada, b_ada)


def _seg_pitch(seg):
    return seg if (seg // SUBLANES) % 2 == 1 else seg + SUBLANES


def _mix_in_kernel(x_ref, mod_ref, w_ref, hist_ref, h0_ref, cw_ref, cb_ref, wg_ref, bg_ref,
                   lam_ref, z_ref, ya_ref, conv_ref, hlast_ref, stage, gl_scr, ga_scr, hist_scr,
                   hc_scr, *, tt, reset_first):
    i = pl.program_id(1)
    seg = tt // SUBLANES
    pitch = _seg_pitch(seg)
    n_hist = CONV_W - 1

    @pl.when(i == 0)
    def _():
        hist_scr[...] = hist_ref[...]
        hc_scr[...] = h0_ref[...]

    lam = lam_ref[...]
    half_coef = (-0.5 * LRU_C * LOG2E) * (jnp.maximum(-lam, 0.0)
                                         + jnp.log1p(jnp.exp(-jnp.abs(lam))))
    srow = lax.broadcasted_iota(jnp.int32, (SUBLANES, LRU_BLOCK), 0)

    sh1 = mod_ref[:, 0:D]
    sc1 = mod_ref[:, D:2 * D]
    u = (x_ref[...] * (1.0 + sc1) + sh1).astype(_BF16)

    def project(c0):
        return jnp.dot(u, w_ref[:, c0:c0 + PROJ_W], preferred_element_type=_F32)

    def project_rnn(k):
        c0 = k * PROJ_W
        xr = project(COL_XR * D + c0)
        conv_ref[:, c0:c0 + PROJ_W] = xr[tt - n_hist:tt, :]
        for m in range(PROJ_W // LRU_BLOCK):
            for s in range(SUBLANES):
                stage[k * (PROJ_W // LRU_BLOCK) + m, s * pitch:s * pitch + seg, :] = (
                    xr[s * seg:(s + 1) * seg, m * LRU_BLOCK:(m + 1) * LRU_BLOCK])
        gl_scr[:, c0:c0 + PROJ_W] = project(COL_GL * D + c0)
        ga_scr[:, c0:c0 + PROJ_W] = project(COL_GA * D + c0)

    z_rows = tt // Z_ROW_SPLIT if tt >= Z_ROW_SPLIT * Z_MIN_ROWS else tt

    def project_z(k):
        kc, kr = divmod(k, tt // z_rows)
        col, off = divmod(kc * PROJ_W, D)
        c0 = Z_SOURCE_COLS[col] * D + off
        rows = slice(kr * z_rows, (kr + 1) * z_rows)
        zk = jnp.dot(u[rows], w_ref[:, c0:c0 + PROJ_W], preferred_element_type=_F32)
        if Z_SOURCE_COLS[col] == COL_Q:
            zk = zk * Q_SCALE
        elif Z_SOURCE_COLS[col] == COL_GB:
            zk = 0.5 * jnp.tanh(zk) + 0.5
        z_ref[rows, kc * PROJ_W:(kc + 1) * PROJ_W] = zk.astype(_BF16)

    def step_regs(v):
        return [v[g * SUBLANES:(g + 1) * SUBLANES] for g in range(seg)]

    def lru_block(n):
        cols = slice(n * LRU_BLOCK, (n + 1) * LRU_BLOCK)
        xg = [stage[n, pl.ds(g, SUBLANES, stride=pitch), :] for g in range(seg)]
        xt = jnp.concatenate(xg, axis=0)

        pre = []
        for j in range(1, n_hist + 1):
            hrow = jnp.broadcast_to(hist_scr[n_hist - j:n_hist - j + 1, cols], (SUBLANES, LRU_BLOCK))
            pre.append(jnp.where(srow == 0, hrow, pltpu.roll(xg[seg - j], 1, 0)))

        def back(j):
            head = [pre[k - 1] for k in range(j, 0, -1)]
            return jnp.concatenate(head + [xt[:tt - j * SUBLANES]], axis=0)

        xc = cb_ref[:, cols] + back(n_hist) * cw_ref[0:1, cols]
        for j in range(1, n_hist):
            xc = xc + back(n_hist - j) * cw_ref[j:j + 1, cols]
        xc = xc + xt * cw_ref[n_hist:CONV_W, cols]

        yield
        gates = jnp.dot(xc.astype(_BF16), wg_ref[n], preferred_element_type=_F32) + bg_ref[n]
        coef = half_coef[:, cols]
        a = jnp.exp2(coef * jnp.tanh(gates[:, :LRU_BLOCK]) + coef)
        mult = jnp.exp2(0.5 * jnp.log2(1.0 - a * a))
        half_xc = 0.5 * xc
        ix = half_xc * jnp.tanh(gates[:, LRU_BLOCK:]) + half_xc
        av = step_regs(a)
        bv = step_regs(mult * ix)
        if reset_first:
            first = jnp.logical_and(srow == 0, i == 0)
            av[0] = jnp.where(first, 0.0, av[0])
            bv[0] = jnp.where(first, ix[:SUBLANES], bv[0])

        yield
        seg_a, seg_b = av[0], bv[0]
        for g in range(1, seg):
            seg_b = av[g] * seg_b + bv[g]
            seg_a = av[g] * seg_a
            if g == seg // 2:
                yield
        for d in (1, 2, 4):
            keep = srow >= d
            a_sh = jnp.where(keep, pltpu.roll(seg_a, d, 0), 1.0)
            b_sh = jnp.where(keep, pltpu.roll(seg_b, d, 0), 0.0)
            seg_b = seg_a * b_sh + seg_b
            seg_a = seg_a * a_sh
        yield
        h_in = jnp.broadcast_to(hc_scr[:, cols], (SUBLANES, LRU_BLOCK))
        h_end = seg_a * h_in + seg_b
        h = jnp.where(srow == 0, h_in, pltpu.roll(h_end, 1, 0))
        for g in range(seg):
            h = av[g] * h + bv[g]
            stage[n, pl.ds(g, SUBLANES, stride=pitch), :] = h
            if g == seg // 2:
                yield
        hc_scr[:, cols] = h[SUBLANES - 1:SUBLANES]
        hlast_ref[:, cols] = h[SUBLANES - 1:SUBLANES]

        yield
        for s in range(SUBLANES):
            rows = slice(s * seg, (s + 1) * seg)
            gl = gl_scr[rows, cols]
            hq = stage[n, s * pitch:s * pitch + seg, :] * (0.25 * gl)
            w = hq + hq * jnp.tanh(gl * (GELU_C1 + GELU_C2 * (gl * gl)))
            ya_ref[rows, cols] = (w + w * jnp.tanh(ga_scr[rows, cols])).astype(_BF16)
            if s == SUBLANES // 2 - 1:
                yield

    blocks_per_chunk = PROJ_W // LRU_BLOCK
    n_z = (D_Z // PROJ_W) * (tt // z_rows)
    n_slots = N_LRU_BLOCKS * LRU_STAGES
    slot = 0
    project_rnn(0)
    for n in range(N_LRU_BLOCKS):
        if n % blocks_per_chunk == 0 and n + blocks_per_chunk < N_LRU_BLOCKS:
            project_rnn(n // blocks_per_chunk + 1)
        for _ in lru_block(n):
            for k in range(slot * n_z // n_slots, (slot + 1) * n_z // n_slots):
                project_z(k)
            slot += 1
    assert slot == n_slots

    hist_scr[...] = conv_ref[...]


def _mix_in(x2, mod3, w_in, hist, h0, conv_w, conv_b, wg, bg, lam, bsz, t, tt, reset_first):
    nt = t // tt
    kern = functools.partial(_mix_in_kernel, tt=tt, reset_first=reset_first)
    return pl.pallas_call(
        kern,
        grid=(bsz, nt),
        in_specs=[
            pl.BlockSpec((tt, D), lambda b, i: (b * nt + i, 0)),
            pl.BlockSpec((None, 1, 6 * D), lambda b, i: (b, 0, 0)),
            _resident((D, D_IN)),
            pl.BlockSpec((None, CONV_W - 1, D), lambda b, i: (b, 0, 0)),
            pl.BlockSpec((None, 1, D), lambda b, i: (b, 0, 0)),
            _resident((CONV_W, D)),
            _resident((1, D)),
            _resident((N_LRU_BLOCKS, LRU_BLOCK, 2 * LRU_BLOCK)),
            _resident((N_LRU_BLOCKS, 1, 2 * LRU_BLOCK)),
            _resident((1, D)),
        ],
        out_specs=[
            pl.BlockSpec((tt, D_Z), lambda b, i: (b * nt + i, 0)),
            pl.BlockSpec((tt, D), lambda b, i: (b * nt + i, 0)),
            pl.BlockSpec((None, CONV_W - 1, D), lambda b, i: (b, 0, 0)),
            pl.BlockSpec((None, 1, D), lambda b, i: (b, 0, 0)),
        ],
        out_shape=[
            jax.ShapeDtypeStruct((bsz * t, D_Z), _BF16),
            jax.ShapeDtypeStruct((bsz * t, D), _BF16),
            jax.ShapeDtypeStruct((bsz, CONV_W - 1, D), _F32),
            jax.ShapeDtypeStruct((bsz, 1, D), _F32),
        ],
        scratch_shapes=[
            pltpu.VMEM((N_LRU_BLOCKS, SUBLANES * _seg_pitch(tt // SUBLANES), LRU_BLOCK), _F32),
            pltpu.VMEM((tt, D), _F32),
            pltpu.VMEM((tt, D), _F32),
            pltpu.VMEM((CONV_W - 1, D), _F32),
            pltpu.VMEM((1, D), _F32),
        ],
        compiler_params=pltpu.CompilerParams(
            dimension_semantics=("arbitrary", "arbitrary"), vmem_limit_bytes=VMEM_LIMIT),
        name="mix_in",
    )(x2, mod3, w_in, hist, h0, conv_w, conv_b, wg, bg, lam)


def _attn_out_kernel(q_ref, kp_ref, kc_ref, vp_ref, vc_ref, sgb_ref, row0_ref,
                     x_ref, ya_ref, mod_ref, wo_ref, wu_ref, bu_ref, wd_ref, bd_ref, ln_ref,
                     y_ref, bias_scr, sc_scr, yb_scr, *, tm, sb, band, nt):
    j = pl.program_id(0)
    i = jnp.minimum(j, pl.num_programs(0) - 2) % nt
    past = kp_ref.shape[0]
    win = past + sb
    nsub = tm // sb
    slot_w = j % 2
    slot_r = 1 - slot_w

    @pl.when(j == 0)
    def _first_step():
        yb_scr[...] = jnp.zeros_like(yb_scr)
        qq = lax.broadcasted_iota(jnp.int32, (sb, win), 0)
        kk = lax.broadcasted_iota(jnp.int32, (sb, win), 1)
        if band:
            dc = (kk // CHUNK) - (qq // CHUNK)
            ok = jnp.logical_and(dc >= 0, dc <= BAND_ROWS // CHUNK)
        else:
            ok = kk < past + CHUNK
        for h in range(N_HEADS):
            base = jnp.broadcast_to(row0_ref[h:h + 1, :], (sb, ROLL_W))
            toep = pltpu.roll(base, 0, 1, stride=1, stride_axis=0)[:, :win] * LOG2E
            bias_scr[h] = jnp.where(ok, toep, NEG_INF)
        bias_scr[N_HEADS] = jnp.full((sb, win), NEG_INF, _F32)

    def rows(prev_ref, cur_ref, so, cols):
        parts = []
        if so < past:
            parts.append(prev_ref[so:past, cols])
        parts.append(cur_ref[max(so - past, 0):so + sb, cols])
        return parts[0] if len(parts) == 1 else jnp.concatenate(parts, axis=0)

    def scores(n, s, h):
        cols = slice(h * HEAD_DIM, (h + 1) * HEAD_DIM)
        qh = q_ref[s * sb:(s + 1) * sb, cols]
        kh = rows(kp_ref, kc_ref, s * sb, cols)
        sc = lax.dot_general(qh, kh, (((1,), (1,)), ((), ())), preferred_element_type=_F32)
        before_start = past - s * sb if band else 0
        biased = []
        for c0 in range(0, win, HEAD_DIM):
            lanes = slice(c0, c0 + HEAD_DIM)
            slot = jnp.where(i == 0, N_HEADS, h) if c0 + HEAD_DIM <= before_start else h
            biased.append(sc[:, lanes] + bias_scr[slot, :, lanes])
        sc_scr[n % SC_SLOTS] = jnp.concatenate(biased, axis=1)

    def probs(n, s, h):
        sc = sc_scr[n % SC_SLOTS]
        return jnp.exp2(sc - jnp.max(sc, axis=-1, keepdims=True)).astype(_BF16)

    ones_cols = jnp.ones((win, HEAD_DIM), _BF16)

    def weighted_values(s, h, p):
        cols = slice(h * HEAD_DIM, (h + 1) * HEAD_DIM)
        vh = jnp.concatenate([rows(vp_ref, vc_ref, s * sb, cols), ones_cols], axis=1)
        ol = jnp.dot(p, vh, preferred_element_type=_F32)
        gate = sgb_ref[s * sb:(s + 1) * sb, cols].astype(_F32)
        scale = gate * (1.0 / ol[:, HEAD_DIM:])
        yb_scr[slot_w, s * sb:(s + 1) * sb, cols] = (ol[:, :HEAD_DIM] * scale).astype(_BF16)

    def attention_stages():
        pairs = [(s, h) for s in range(nsub) for h in range(N_HEADS)]
        for n in range(min(SC_AHEAD, len(pairs))):
            scores(n, *pairs[n])
        ready = probs(0, *pairs[0])
        for n, (s, h) in enumerate(pairs):
            yield
            if n + SC_AHEAD < len(pairs):
                scores(n + SC_AHEAD, *pairs[n + SC_AHEAD])
            current = ready
            if n + 1 < len(pairs):
                ready = probs(n + 1, *pairs[n + 1])
            weighted_values(s, h, current)

    g1 = mod_ref[:, 2 * D:3 * D]
    sh2 = mod_ref[:, 3 * D:4 * D]
    sc2 = mod_ref[:, 4 * D:5 * D]
    g2 = mod_ref[:, 5 * D:6 * D]

    def token_stages(rows):
        yb = yb_scr[slot_r, rows, :]
        merged = (ya_ref[rows, :].astype(_F32) + yb.astype(_F32)).astype(_BF16)
        yield
        o = jnp.dot(merged, wo_ref[...], preferred_element_type=_F32)
        yield
        x1 = _layer_norm(ALPHA * x_ref[rows, :] + (1.0 + g1) * o, ln_ref[0:1, :], ln_ref[1:2, :])
        u2 = (x1 * (1.0 + sc2) + sh2).astype(_BF16)
        yield
        f = bd_ref[...]
        for c in range(D_FF // FF_CHUNK):
            cols = slice(c * FF_CHUNK, (c + 1) * FF_CHUNK)
            hid = jnp.dot(u2, wu_ref[:, cols], preferred_element_type=_F32) + bu_ref[:, cols]
            hid = jnp.square(jnp.maximum(hid, 0.0)).astype(_BF16)
            yield
            f = f + jnp.dot(hid, wd_ref[cols, :], preferred_element_type=_F32)
            yield
        y_ref[rows, :] = _layer_norm(ALPHA * x1 + (1.0 + g2) * f, ln_ref[2:3, :], ln_ref[3:4, :])

    def staggered(gens):
        live = list(range(len(gens)))
        tick = 0
        while True:
            for k in list(live):
                if tick >= k and next(gens[k], _DONE) is _DONE:
                    live.remove(k)
            tick += 1
            if not live:
                return
            yield

    parts = TOKEN_GROUP_PARTS if tm >= sum(TOKEN_GROUP_PARTS) * TOKEN_MIN_PART_ROWS else (1,)
    unit = tm // sum(parts)
    bounds = [unit * sum(parts[:k]) for k in range(len(parts) + 1)]
    tokens = staggered([token_stages(slice(bounds[k], bounds[k + 1]))
                        for k in range(len(parts))])
    attention = attention_stages()
    n_att = nsub * N_HEADS + 1
    n_tok = len(parts) - 1 + 4 + 2 * (D_FF // FF_CHUNK)
    done_att = done_tok = 0
    while done_att < n_att or done_tok < n_tok:
        if done_tok == n_tok or (done_att < n_att and done_att * n_tok <= done_tok * n_att):
            next(attention, _DONE)
            done_att += 1
        else:
            next(tokens, _DONE)
            done_tok += 1
    assert next(attention, _DONE) is _DONE and next(tokens, _DONE) is _DONE


def _attn_out(z, kp_arr, vp_arr, sample_cache, row0, x2, ya, mod3, w_out, w_up, b_up, w_down,
              b_down, ln, bsz, t, tm, sb, past, band):
    nt = t // tm
    n_tiles = bsz * nt
    win = past + sb
    kern = functools.partial(_attn_out_kernel, tm=tm, sb=sb, band=band, nt=nt)

    def att_tile(j):
        return jnp.minimum(j, n_tiles - 1)

    def tok_tile(j):
        return jnp.maximum(j - 1, 0)

    def prev_rows(col):
        if sample_cache:
            return lambda j: (att_tile(j), 0)
        return lambda j: (jnp.where(att_tile(j) % nt == 0, att_tile(j), att_tile(j) - 1), col)

    return pl.pallas_call(
        kern,
        grid=(n_tiles + 1,),
        in_specs=[
            pl.BlockSpec((tm, D), lambda j: (att_tile(j), Z_Q)),
            pl.BlockSpec((past, D), prev_rows(Z_K)),
            pl.BlockSpec((tm, D), lambda j: (att_tile(j), Z_K)),
            pl.BlockSpec((past, D), prev_rows(Z_V)),
            pl.BlockSpec((tm, D), lambda j: (att_tile(j), Z_V)),
            pl.BlockSpec((tm, D), lambda j: (att_tile(j), Z_GB)),
            _resident((N_HEADS, ROLL_W)),
            pl.BlockSpec((tm, D), lambda j: (tok_tile(j), 0)),
            pl.BlockSpec((tm, D), lambda j: (tok_tile(j), 0)),
            pl.BlockSpec((None, 1, 6 * D), lambda j: (tok_tile(j) // nt, 0, 0)),
            _resident((D, D)),
            _resident((D, D_FF)),
            _resident((1, D_FF)),
            _resident((D_FF, D)),
            _resident((1, D)),
            _resident((4, D)),
        ],
        out_specs=pl.BlockSpec((tm, D), lambda j: (tok_tile(j), 0)),
        out_shape=jax.ShapeDtypeStruct((bsz * t, D), _F32),
        scratch_shapes=[
            pltpu.VMEM((N_HEADS + 1, sb, win), _F32),
            pltpu.VMEM((SC_SLOTS, sb, win), _F32),
            pltpu.VMEM((2, tm, D), _BF16),
        ],
        compiler_params=pltpu.CompilerParams(
            dimension_semantics=("arbitrary",), vmem_limit_bytes=VMEM_LIMIT_ATTN_OUT),
        name="attn_out",
    )(z, kp_arr, z, vp_arr, z, z, row0, x2, ya, mod3, w_out, w_up, b_up, w_down, b_down, ln)


def _rel_bias_row0(table, past):
    far = jnp.broadcast_to(table[:, 2 * MAX_REL:], (N_HEADS, past - MAX_REL + 1))
    near = table[:, 2 * MAX_REL - 1::-1]
    tail = jnp.broadcast_to(table[:, 2 * MAX_REL:],
                            (N_HEADS, ROLL_W - (past + MAX_REL + 1)))
    return jnp.concatenate([far, near, tail], axis=1).astype(_F32)


def _heads(a, bsz, rows):
    return a.reshape(1, bsz, rows, N_HEADS, HEAD_DIM)


@jax.jit
def _forward(x_prompt, x_sample, c_prompt, c_sample, cache_k, cache_v, state_conv, state_lru,
             w_ada, b_ada, w_in, conv_w, conv_b, w_rg, b_rg, w_ig, b_ig, lru_lambda, rel_bias,
             w_out, ln1_g, ln1_b, w_up, b_up, w_down, b_down, ln2_g, ln2_b):
    bp, t, _ = x_prompt.shape
    bs, s, _ = x_sample.shape
    l = 0
    col_scale = jnp.where(jnp.arange(D_IN) >= COL_GA * D, 0.5, 1.0).astype(_F32)
    w_in_b = (w_in[l] * col_scale[None, :]).astype(_BF16)
    w_out_b = w_out[l].astype(_BF16)
    w_up_b = w_up[l].astype(_BF16)
    w_down_b = w_down[l].astype(_BF16)
    wg = (0.5 * jnp.concatenate([w_rg[l], w_ig[l]], axis=-1)).astype(_BF16)
    bg = 0.5 * jnp.concatenate([b_rg[l], b_ig[l]], axis=-1)[:, None, :]
    lam = lru_lambda[l][None, :]
    cb = conv_b[l][None, :]
    ln = jnp.stack([ln1_g[l], ln1_b[l], ln2_g[l], ln2_b[l]])
    bu = b_up[l][None, :]
    bd = b_down[l][None, :]

    mod = _ada(jnp.concatenate([c_prompt, c_sample], axis=0), w_ada[l].astype(_BF16),
               b_ada[l][None, :])
    mod_p = mod[:bp, None, :]
    mod_s = mod[bp:, None, :]

    tm = 512
    xp2 = x_prompt.reshape(bp * t, D)
    z, ya, conv_p, h_p = _mix_in(xp2, mod_p, w_in_b, jnp.zeros((bp, CONV_W - 1, D), _F32),
                                 jnp.zeros((bp, 1, D), _F32), conv_w[l], cb, wg, bg, lam,
                                 bp, t, tm, True)
    y_p = _attn_out(z, z, z, False, _rel_bias_row0(rel_bias[l], BAND_ROWS), xp2, ya, mod_p,
                    w_out_b, w_up_b, bu, w_down_b, bd, ln, bp, t, tm, 2 * CHUNK, BAND_ROWS, True)
    z3 = z.reshape(bp, t, D_Z)
    rows = min(BAND_ROWS, t)
    k_p = z3[:, t - rows:, Z_K * D:(Z_K + 1) * D].astype(_F32)
    v_p = z3[:, t - rows:, Z_V * D:(Z_V + 1) * D].astype(_F32)

    xs2 = x_sample.reshape(bs * s, D)
    zs, ya_s, conv_s, h_s = _mix_in(xs2, mod_s, w_in_b, state_conv[l], state_lru[l][:, None, :],
                                    conv_w[l], cb, wg, bg, lam, bs, s, s, False)
    sbs = 2 * CHUNK
    zs3 = zs.reshape(bs, s, D_Z)

    def pad_rows(a):
        a3 = a.reshape(bs, s, a.shape[-1])
        return jnp.pad(a3, ((0, 0), (0, sbs - s), (0, 0))).reshape(bs * sbs, a.shape[-1])

    n_past = cache_k.shape[2]
    ck = cache_k[l].reshape(bs * n_past, D).astype(_BF16)
    cv = cache_v[l].reshape(bs * n_past, D).astype(_BF16)
    y_s = _attn_out(pad_rows(zs), ck, cv, True, _rel_bias_row0(rel_bias[l], n_past),
                    pad_rows(xs2), pad_rows(ya_s), mod_s, w_out_b, w_up_b, bu, w_down_b, bd, ln,
                    bs, sbs, sbs, sbs, n_past, False)
    y_s = y_s.reshape(bs, sbs, D)[:, :s]
    k_s = zs3[:, :, Z_K * D:(Z_K + 1) * D].astype(_F32)
    v_s = zs3[:, :, Z_V * D:(Z_V + 1) * D].astype(_F32)

    return (y_p.reshape(bp, t, D), y_s.reshape(bs, s, D),
            _heads(k_p, bp, rows), _heads(v_p, bp, rows), conv_p[None], h_p.reshape(1, bp, D),
            _heads(k_s, bs, s), _heads(v_s, bs, s), conv_s[None], h_s.reshape(1, bs, D))


def kernel(x_prompt, x_sample, c_prompt, c_sample, cache_k, cache_v, state_conv, state_lru, w_ada, b_ada, w_in, conv_w, conv_b, w_rg, b_rg, w_ig, b_ig, lru_lambda, rel_bias, w_out, ln1_g, ln1_b, w_up, b_up, w_down, b_down, ln2_g, ln2_b):
    return _forward(x_prompt, x_sample, c_prompt, c_sample, cache_k, cache_v, state_conv,
                    state_lru, w_ada, b_ada, w_in, conv_w, conv_b, w_rg, b_rg, w_ig, b_ig,
                    lru_lambda, rel_bias, w_out, ln1_g, ln1_b, w_up, b_up, w_down, b_down,
                    ln2_g, ln2_b)
```

```python
import functools
import math

import jax
import jax.numpy as jnp
from jax import lax
from jax.experimental import pallas as pl
from jax.experimental.pallas import tpu as pltpu

D = 1024
CHUNK = 64
BAND_ROWS = 8 * CHUNK
N_HEADS = 8
HEAD_DIM = D // N_HEADS
N_LRU_BLOCKS = 8
LRU_BLOCK = D // N_LRU_BLOCKS
CONV_W = 4
LRU_C = 8.0
MAX_REL = 128
ATT_SCALE = HEAD_DIM ** -0.5
NEG_INF = -1e30
D_FF = 4 * D
D_IN = 7 * D
DEPTH = 1
ALPHA = (2 * DEPTH) ** 0.25
LN_EPS = 1e-5

LOG2E = math.log2(math.e)
Q_SCALE = ATT_SCALE * LOG2E
ROLL_W = 768
SC_AHEAD = 6
SC_SLOTS = SC_AHEAD + 1
GELU_C1 = math.sqrt(2.0 / math.pi)
GELU_C2 = GELU_C1 * 0.044715

SUBLANES = 8
VMEM_LIMIT = 56 * 1024 * 1024
VMEM_LIMIT_ATTN_OUT = 62 * 1024 * 1024

COL_XR, COL_GL, COL_Q, COL_K, COL_V, COL_GA, COL_GB = range(7)
Z_SOURCE_COLS = (COL_Q, COL_K, COL_V, COL_GB)
Z_Q, Z_K, Z_V, Z_GB = range(4)
D_Z = len(Z_SOURCE_COLS) * D
PROJ_W = 256
Z_ROW_SPLIT = 2
Z_MIN_ROWS = 256
LRU_STAGES = 7
TOKEN_GROUP_PARTS = (1, 1)
TOKEN_MIN_PART_ROWS = 256
FF_CHUNK = 512

_F32 = jnp.float32
_BF16 = jnp.bfloat16
_DONE = object()


def _resident(shape):
    n = len(shape)
    return pl.BlockSpec(shape, lambda *_: (0,) * n, pipeline_mode=pl.Buffered(1))


def _sigmoid(v):
    return 0.5 * jnp.tanh(0.5 * v) + 0.5


def _layer_norm(v, g, b):
    mu = jnp.mean(v, axis=-1, keepdims=True)
    vc = v - mu
    var = jnp.mean(vc * vc, axis=-1, keepdims=True)
    return vc * lax.rsqrt(var + LN_EPS) * g + b


def _ada_kernel(c_ref, w_ref, b_ref, o_ref):
    c = c_ref[...]
    s = (c * _sigmoid(c)).astype(_BF16)
    o_ref[...] = jnp.dot(s, w_ref[...], preferred_element_type=_F32) + b_ref[...]


def _ada(c, w_ada, b_ada):
    n = c.shape[0]
    return pl.pallas_call(
        _ada_kernel,
        out_shape=jax.ShapeDtypeStruct((n, 6 * D), _F32),
        compiler_params=pltpu.CompilerParams(vmem_limit_bytes=VMEM_LIMIT),
        name="ada",
    )(c, w_ada, b_ada)


def _seg_pitch(seg):
    return seg if (seg // SUBLANES) % 2 == 1 else seg + SUBLANES


def _mix_in_kernel(x_ref, mod_ref, w_ref, hist_ref, h0_ref, cw_ref, cb_ref, wg_ref, bg_ref,
                   lam_ref, z_ref, ya_ref, conv_ref, hlast_ref, stage, gl_scr, ga_scr, hist_scr,
                   hc_scr, *, tt, reset_first):
    i = pl.program_id(1)
    seg = tt // SUBLANES
    pitch = _seg_pitch(seg)
    n_hist = CONV_W - 1

    @pl.when(i == 0)
    def _():
        hist_scr[...] = hist_ref[...]
        hc_scr[...] = h0_ref[...]

    lam = lam_ref[...]
    half_coef = (-0.5 * LRU_C * LOG2E) * (jnp.maximum(-lam, 0.0)
                                         + jnp.log1p(jnp.exp(-jnp.abs(lam))))
    srow = lax.broadcasted_iota(jnp.int32, (SUBLANES, LRU_BLOCK), 0)

    sh1 = mod_ref[:, 0:D]
    sc1 = mod_ref[:, D:2 * D]
    u = (x_ref[...] * (1.0 + sc1) + sh1).astype(_BF16)

    def project(c0):
        return jnp.dot(u, w_ref[:, c0:c0 + PROJ_W], preferred_element_type=_F32)

    def project_rnn(k):
        c0 = k * PROJ_W
        xr = project(COL_XR * D + c0)
        conv_ref[:, c0:c0 + PROJ_W] = xr[tt - n_hist:tt, :]
        for m in range(PROJ_W // LRU_BLOCK):
            for s in range(SUBLANES):
                stage[k * (PROJ_W // LRU_BLOCK) + m, s * pitch:s * pitch + seg, :] = (
                    xr[s * seg:(s + 1) * seg, m * LRU_BLOCK:(m + 1) * LRU_BLOCK])
        gl_scr[:, c0:c0 + PROJ_W] = project(COL_GL * D + c0)
        ga_scr[:, c0:c0 + PROJ_W] = project(COL_GA * D + c0)

    z_rows = tt // Z_ROW_SPLIT if tt >= Z_ROW_SPLIT * Z_MIN_ROWS else tt

    def project_z(k):
        kc, kr = divmod(k, tt // z_rows)
        col, off = divmod(kc * PROJ_W, D)
        c0 = Z_SOURCE_COLS[col] * D + off
        rows = slice(kr * z_rows, (kr + 1) * z_rows)
        zk = jnp.dot(u[rows], w_ref[:, c0:c0 + PROJ_W], preferred_element_type=_F32)
        if Z_SOURCE_COLS[col] == COL_Q:
            zk = zk * Q_SCALE
        elif Z_SOURCE_COLS[col] == COL_GB:
            zk = 0.5 * jnp.tanh(zk) + 0.5
        z_ref[rows, kc * PROJ_W:(kc + 1) * PROJ_W] = zk.astype(_BF16)

    def step_regs(v):
        return [v[g * SUBLANES:(g + 1) * SUBLANES] for g in range(seg)]

    def lru_block(n):
        cols = slice(n * LRU_BLOCK, (n + 1) * LRU_BLOCK)
        xg = [stage[n, pl.ds(g, SUBLANES, stride=pitch), :] for g in range(seg)]
        xt = jnp.concatenate(xg, axis=0)

        pre = []
        for j in range(1, n_hist + 1):
            hrow = jnp.broadcast_to(hist_scr[n_hist - j:n_hist - j + 1, cols], (SUBLANES, LRU_BLOCK))
            pre.append(jnp.where(srow == 0, hrow, pltpu.roll(xg[seg - j], 1, 0)))

        def back(j):
            head = [pre[k - 1] for k in range(j, 0, -1)]
            return jnp.concatenate(head + [xt[:tt - j * SUBLANES]], axis=0)

        xc = cb_ref[:, cols] + back(n_hist) * cw_ref[0:1, cols]
        for j in range(1, n_hist):
            xc = xc + back(n_hist - j) * cw_ref[j:j + 1, cols]
        xc = xc + xt * cw_ref[n_hist:CONV_W, cols]

        yield
        gates = jnp.dot(xc.astype(_BF16), wg_ref[n], preferred_element_type=_F32) + bg_ref[n]
        coef = half_coef[:, cols]
        a = jnp.exp2(coef * jnp.tanh(gates[:, :LRU_BLOCK]) + coef)
        mult = jnp.exp2(0.5 * jnp.log2(1.0 - a * a))
        half_xc = 0.5 * xc
        ix = half_xc * jnp.tanh(gates[:, LRU_BLOCK:]) + half_xc
        av = step_regs(a)
        bv = step_regs(mult * ix)
        if reset_first:
            first = jnp.logical_and(srow == 0, i == 0)
            av[0] = jnp.where(first, 0.0, av[0])
            bv[0] = jnp.where(first, ix[:SUBLANES], bv[0])

        yield
        seg_a, seg_b = av[0], bv[0]
        for g in range(1, seg):
            seg_b = av[g] * seg_b + bv[g]
            seg_a = av[g] * seg_a
            if g == seg // 2:
                yield
        for d in (1, 2, 4):
            keep = srow >= d
            a_sh = jnp.where(keep, pltpu.roll(seg_a, d, 0), 1.0)
            b_sh = jnp.where(keep, pltpu.roll(seg_b, d, 0), 0.0)
            seg_b = seg_a * b_sh + seg_b
            seg_a = seg_a * a_sh
        yield
        h_in = jnp.broadcast_to(hc_scr[:, cols], (SUBLANES, LRU_BLOCK))
        h_end = seg_a * h_in + seg_b
        h = jnp.where(srow == 0, h_in, pltpu.roll(h_end, 1, 0))
        for g in range(seg):
            h = av[g] * h + bv[g]
            stage[n, pl.ds(g, SUBLANES, stride=pitch), :] = h
            if g == seg // 2:
                yield
        hc_scr[:, cols] = h[SUBLANES - 1:SUBLANES]
        hlast_ref[:, cols] = h[SUBLANES - 1:SUBLANES]

        yield
        for s in range(SUBLANES):
            rows = slice(s * seg, (s + 1) * seg)
            gl = gl_scr[rows, cols]
            hq = stage[n, s * pitch:s * pitch + seg, :] * (0.25 * gl)
            w = hq + hq * jnp.tanh(gl * (GELU_C1 + GELU_C2 * (gl * gl)))
            ya_ref[rows, cols] = (w + w * jnp.tanh(ga_scr[rows, cols])).astype(_BF16)
            if s == SUBLANES // 2 - 1:
                yield

    blocks_per_chunk = PROJ_W // LRU_BLOCK
    n_z = (D_Z // PROJ_W) * (tt // z_rows)
    n_slots = N_LRU_BLOCKS * LRU_STAGES
    slot = 0
    project_rnn(0)
    for n in range(N_LRU_BLOCKS):
        if n % blocks_per_chunk == 0 and n + blocks_per_chunk < N_LRU_BLOCKS:
            project_rnn(n // blocks_per_chunk + 1)
        for _ in lru_block(n):
            for k in range(slot * n_z // n_slots, (slot + 1) * n_z // n_slots):
                project_z(k)
            slot += 1
    assert slot == n_slots

    hist_scr[...] = conv_ref[...]


def _mix_in(x2, mod3, w_in, hist, h0, conv_w, conv_b, wg, bg, lam, bsz, t, tt, reset_first):
    nt = t // tt
    kern = functools.partial(_mix_in_kernel, tt=tt, reset_first=reset_first)
    return pl.pallas_call(
        kern,
        grid=(bsz, nt),
        in_specs=[
            pl.BlockSpec((tt, D), lambda b, i: (b * nt + i, 0)),
            pl.BlockSpec((None, 1, 6 * D), lambda b, i: (b, 0, 0)),
            _resident((D, D_IN)),
            pl.BlockSpec((None, CONV_W - 1, D), lambda b, i: (b, 0, 0)),
            pl.BlockSpec((None, 1, D), lambda b, i: (b, 0, 0)),
            _resident((CONV_W, D)),
            _resident((1, D)),
            _resident((N_LRU_BLOCKS, LRU_BLOCK, 2 * LRU_BLOCK)),
            _resident((N_LRU_BLOCKS, 1, 2 * LRU_BLOCK)),
            _resident((1, D)),
        ],
        out_specs=[
            pl.BlockSpec((tt, D_Z), lambda b, i: (b * nt + i, 0)),
            pl.BlockSpec((tt, D), lambda b, i: (b * nt + i, 0)),
            pl.BlockSpec((None, CONV_W - 1, D), lambda b, i: (b, 0, 0)),
            pl.BlockSpec((None, 1, D), lambda b, i: (b, 0, 0)),
        ],
        out_shape=[
            jax.ShapeDtypeStruct((bsz * t, D_Z), _BF16),
            jax.ShapeDtypeStruct((bsz * t, D), _BF16),
            jax.ShapeDtypeStruct((bsz, CONV_W - 1, D), _F32),
            jax.ShapeDtypeStruct((bsz, 1, D), _F32),
        ],
        scratch_shapes=[
            pltpu.VMEM((N_LRU_BLOCKS, SUBLANES * _seg_pitch(tt // SUBLANES), LRU_BLOCK), _F32),
            pltpu.VMEM((tt, D), _F32),
            pltpu.VMEM((tt, D), _F32),
            pltpu.VMEM((CONV_W - 1, D), _F32),
            pltpu.VMEM((1, D), _F32),
        ],
        compiler_params=pltpu.CompilerParams(
            dimension_semantics=("arbitrary", "arbitrary"), vmem_limit_bytes=VMEM_LIMIT),
        name="mix_in",
    )(x2, mod3, w_in, hist, h0, conv_w, conv_b, wg, bg, lam)


def _attn_out_kernel(q_ref, kp_ref, kc_ref, vp_ref, vc_ref, sgb_ref, row0_ref,
                     x_ref, ya_ref, mod_ref, wo_ref, wu_ref, bu_ref, wd_ref, bd_ref, ln_ref,
                     y_ref, bias_scr, sc_scr, yb_scr, *, tm, sb, band, nt):
    j = pl.program_id(0)
    i = jnp.minimum(j, pl.num_programs(0) - 2) % nt
    past = kp_ref.shape[0]
    win = past + sb
    nsub = tm // sb
    slot_w = j % 2
    slot_r = 1 - slot_w

    @pl.when(j == 0)
    def _first_step():
        yb_scr[...] = jnp.zeros_like(yb_scr)
        qq = lax.broadcasted_iota(jnp.int32, (sb, win), 0)
        kk = lax.broadcasted_iota(jnp.int32, (sb, win), 1)
        if band:
            dc = (kk // CHUNK) - (qq // CHUNK)
            ok = jnp.logical_and(dc >= 0, dc <= BAND_ROWS // CHUNK)
        else:
            ok = kk < past + CHUNK
        for h in range(N_HEADS):
            base = jnp.broadcast_to(row0_ref[h:h + 1, :], (sb, ROLL_W))
            toep = pltpu.roll(base, 0, 1, stride=1, stride_axis=0)[:, :win] * LOG2E
            bias_scr[h] = jnp.where(ok, toep, NEG_INF)
        bias_scr[N_HEADS] = jnp.full((sb, win), NEG_INF, _F32)

    def rows(prev_ref, cur_ref, so, cols):
        parts = []
        if so < past:
            parts.append(prev_ref[so:past, cols])
        parts.append(cur_ref[max(so - past, 0):so + sb, cols])
        return parts[0] if len(parts) == 1 else jnp.concatenate(parts, axis=0)

    def scores(n, s, h):
        cols = slice(h * HEAD_DIM, (h + 1) * HEAD_DIM)
        qh = q_ref[s * sb:(s + 1) * sb, cols]
        kh = rows(kp_ref, kc_ref, s * sb, cols)
        sc = lax.dot_general(qh, kh, (((1,), (1,)), ((), ())), preferred_element_type=_F32)
        before_start = past - s * sb if band else 0
        biased = []
        for c0 in range(0, win, HEAD_DIM):
            lanes = slice(c0, c0 + HEAD_DIM)
            slot = jnp.where(i == 0, N_HEADS, h) if c0 + HEAD_DIM <= before_start else h
            biased.append(sc[:, lanes] + bias_scr[slot, :, lanes])
        sc_scr[n % SC_SLOTS] = jnp.concatenate(biased, axis=1)

    def probs(n, s, h):
        sc = sc_scr[n % SC_SLOTS]
        return jnp.exp2(sc - jnp.max(sc, axis=-1, keepdims=True)).astype(_BF16)

    ones_cols = jnp.ones((win, HEAD_DIM), _BF16)

    def weighted_values(s, h, p):
        cols = slice(h * HEAD_DIM, (h + 1) * HEAD_DIM)
        vh = jnp.concatenate([rows(vp_ref, vc_ref, s * sb, cols), ones_cols], axis=1)
        ol = jnp.dot(p, vh, preferred_element_type=_F32)
        gate = sgb_ref[s * sb:(s + 1) * sb, cols].astype(_F32)
        scale = gate * (1.0 / ol[:, HEAD_DIM:])
        yb_scr[slot_w, s * sb:(s + 1) * sb, cols] = (ol[:, :HEAD_DIM] * scale).astype(_BF16)

    def attention_stages():
        pairs = [(s, h) for s in range(nsub) for h in range(N_HEADS)]
        for n in range(min(SC_AHEAD, len(pairs))):
            scores(n, *pairs[n])
        ready = probs(0, *pairs[0])
        for n, (s, h) in enumerate(pairs):
            yield
            if n + SC_AHEAD < len(pairs):
                scores(n + SC_AHEAD, *pairs[n + SC_AHEAD])
            current = ready
            if n + 1 < len(pairs):
                ready = probs(n + 1, *pairs[n + 1])
            weighted_values(s, h, current)

    g1 = mod_ref[:, 2 * D:3 * D]
    sh2 = mod_ref[:, 3 * D:4 * D]
    sc2 = mod_ref[:, 4 * D:5 * D]
    g2 = mod_ref[:, 5 * D:6 * D]

    def token_stages(rows):
        yb = yb_scr[slot_r, rows, :]
        merged = (ya_ref[rows, :].astype(_F32) + yb.astype(_F32)).astype(_BF16)
        yield
        o = jnp.dot(merged, wo_ref[...], preferred_element_type=_F32)
        yield
        x1 = _layer_norm(ALPHA * x_ref[rows, :] + (1.0 + g1) * o, ln_ref[0:1, :], ln_ref[1:2, :])
        u2 = (x1 * (1.0 + sc2) + sh2).astype(_BF16)
        yield
        f = bd_ref[...]
        for c in range(D_FF // FF_CHUNK):
            cols = slice(c * FF_CHUNK, (c + 1) * FF_CHUNK)
            hid = jnp.dot(u2, wu_ref[:, cols], preferred_element_type=_F32) + bu_ref[:, cols]
            hid = jnp.square(jnp.maximum(hid, 0.0)).astype(_BF16)
            yield
            f = f + jnp.dot(hid, wd_ref[cols, :], preferred_element_type=_F32)
            yield
        y_ref[rows, :] = _layer_norm(ALPHA * x1 + (1.0 + g2) * f, ln_ref[2:3, :], ln_ref[3:4, :])

    def staggered(gens):
        live = list(range(len(gens)))
        tick = 0
        while True:
            for k in list(live):
                if tick >= k and next(gens[k], _DONE) is _DONE:
                    live.remove(k)
            tick += 1
            if not live:
                return
            yield

    parts = TOKEN_GROUP_PARTS if tm >= sum(TOKEN_GROUP_PARTS) * TOKEN_MIN_PART_ROWS else (1,)
    unit = tm // sum(parts)
    bounds = [unit * sum(parts[:k]) for k in range(len(parts) + 1)]
    tokens = staggered([token_stages(slice(bounds[k], bounds[k + 1]))
                        for k in range(len(parts))])
    attention = attention_stages()
    n_att = nsub * N_HEADS + 1
    n_tok = len(parts) - 1 + 4 + 2 * (D_FF // FF_CHUNK)
    done_att = done_tok = 0
    while done_att < n_att or done_tok < n_tok:
        if done_tok == n_tok or (done_att < n_att and done_att * n_tok <= done_tok * n_att):
            next(attention, _DONE)
            done_att += 1
        else:
            next(tokens, _DONE)
            done_tok += 1
    assert next(attention, _DONE) is _DONE and next(tokens, _DONE) is _DONE


def _attn_out(z, kp_arr, vp_arr, sample_cache, row0, x2, ya, mod3, w_out, w_up, b_up, w_down,
              b_down, ln, bsz, t, tm, sb, past, band):
    nt = t // tm
    n_tiles = bsz * nt
    win = past + sb
    kern = functools.partial(_attn_out_kernel, tm=tm, sb=sb, band=band, nt=nt)

    def att_tile(j):
        return jnp.minimum(j, n_tiles - 1)

    def tok_tile(j):
        return jnp.maximum(j - 1, 0)

    def prev_rows(col):
        if sample_cache:
            return lambda j: (att_tile(j), 0)
        return lambda j: (jnp.where(att_tile(j) % nt == 0, att_tile(j), att_tile(j) - 1), col)

    return pl.pallas_call(
        kern,
        grid=(n_tiles + 1,),
        in_specs=[
            pl.BlockSpec((tm, D), lambda j: (att_tile(j), Z_Q)),
            pl.BlockSpec((past, D), prev_rows(Z_K)),
            pl.BlockSpec((tm, D), lambda j: (att_tile(j), Z_K)),
            pl.BlockSpec((past, D), prev_rows(Z_V)),
            pl.BlockSpec((tm, D), lambda j: (att_tile(j), Z_V)),
            pl.BlockSpec((tm, D), lambda j: (att_tile(j), Z_GB)),
            _resident((N_HEADS, ROLL_W)),
            pl.BlockSpec((tm, D), lambda j: (tok_tile(j), 0)),
            pl.BlockSpec((tm, D), lambda j: (tok_tile(j), 0)),
            pl.BlockSpec((None, 1, 6 * D), lambda j: (tok_tile(j) // nt, 0, 0)),
            _resident((D, D)),
            _resident((D, D_FF)),
            _resident((1, D_FF)),
            _resident((D_FF, D)),
            _resident((1, D)),
            _resident((4, D)),
        ],
        out_specs=pl.BlockSpec((tm, D), lambda j: (tok_tile(j), 0)),
        out_shape=jax.ShapeDtypeStruct((bsz * t, D), _F32),
        scratch_shapes=[
            pltpu.VMEM((N_HEADS + 1, sb, win), _F32),
            pltpu.VMEM((SC_SLOTS, sb, win), _F32),
            pltpu.VMEM((2, tm, D), _BF16),
        ],
        compiler_params=pltpu.CompilerParams(
            dimension_semantics=("arbitrary",), vmem_limit_bytes=VMEM_LIMIT_ATTN_OUT),
        name="attn_out",
    )(z, kp_arr, z, vp_arr, z, z, row0, x2, ya, mod3, w_out, w_up, b_up, w_down, b_down, ln)


def _rel_bias_row0(table, past):
    far = jnp.broadcast_to(table[:, 2 * MAX_REL:], (N_HEADS, past - MAX_REL + 1))
    near = table[:, 2 * MAX_REL - 1::-1]
    tail = jnp.broadcast_to(table[:, 2 * MAX_REL:],
                            (N_HEADS, ROLL_W - (past + MAX_REL + 1)))
    return jnp.concatenate([far, near, tail], axis=1).astype(_F32)


def _heads(a, bsz, rows):
    return a.reshape(1, bsz, rows, N_HEADS, HEAD_DIM)


@jax.jit
def _forward(x_prompt, x_sample, c_prompt, c_sample, cache_k, cache_v, state_conv, state_lru,
             w_ada, b_ada, w_in, conv_w, conv_b, w_rg, b_rg, w_ig, b_ig, lru_lambda, rel_bias,
             w_out, ln1_g, ln1_b, w_up, b_up, w_down, b_down, ln2_g, ln2_b):
    bp, t, _ = x_prompt.shape
    bs, s, _ = x_sample.shape
    l = 0
    col_scale = jnp.where(jnp.arange(D_IN) >= COL_GA * D, 0.5, 1.0).astype(_F32)
    w_in_b = (w_in[l] * col_scale[None, :]).astype(_BF16)
    w_out_b = w_out[l].astype(_BF16)
    w_up_b = w_up[l].astype(_BF16)
    w_down_b = w_down[l].astype(_BF16)
    wg = (0.5 * jnp.concatenate([w_rg[l], w_ig[l]], axis=-1)).astype(_BF16)
    bg = 0.5 * jnp.concatenate([b_rg[l], b_ig[l]], axis=-1)[:, None, :]
    lam = lru_lambda[l][None, :]
    cb = conv_b[l][None, :]
    ln = jnp.stack([ln1_g[l], ln1_b[l], ln2_g[l], ln2_b[l]])
    bu = b_up[l][None, :]
    bd = b_down[l][None, :]

    mod = _ada(jnp.concatenate([c_prompt, c_sample], axis=0), w_ada[l].astype(_BF16),
               b_ada[l][None, :])
    mod_p = mod[:bp, None, :]
    mod_s = mod[bp:, None, :]

    tm = 512
    xp2 = x_prompt.reshape(bp * t, D)
    z, ya, conv_p, h_p = _mix_in(xp2, mod_p, w_in_b, jnp.zeros((bp, CONV_W - 1, D), _F32),
                                 jnp.zeros((bp, 1, D), _F32), conv_w[l], cb, wg, bg, lam,
                                 bp, t, tm, True)
    y_p = _attn_out(z, z, z, False, _rel_bias_row0(rel_bias[l], BAND_ROWS), xp2, ya, mod_p,
                    w_out_b, w_up_b, bu, w_down_b, bd, ln, bp, t, tm, 2 * CHUNK, BAND_ROWS, True)
    z3 = z.reshape(bp, t, D_Z)
    rows = min(BAND_ROWS, t)
    k_p = z3[:, t - rows:, Z_K * D:(Z_K + 1) * D].astype(_F32)
    v_p = z3[:, t - rows:, Z_V * D:(Z_V + 1) * D].astype(_F32)

    xs2 = x_sample.reshape(bs * s, D)
    zs, ya_s, conv_s, h_s = _mix_in(xs2, mod_s, w_in_b, state_conv[l], state_lru[l][:, None, :],
                                    conv_w[l], cb, wg, bg, lam, bs, s, s, False)
    sbs = 2 * CHUNK
    zs3 = zs.reshape(bs, s, D_Z)

    def pad_rows(a):
        a3 = a.reshape(bs, s, a.shape[-1])
        return jnp.pad(a3, ((0, 0), (0, sbs - s), (0, 0))).reshape(bs * sbs, a.shape[-1])

    n_past = cache_k.shape[2]
    ck = cache_k[l].reshape(bs * n_past, D).astype(_BF16)
    cv = cache_v[l].reshape(bs * n_past, D).astype(_BF16)
    y_s = _attn_out(pad_rows(zs), ck, cv, True, _rel_bias_row0(rel_bias[l], n_past),
                    pad_rows(xs2), pad_rows(ya_s), mod_s, w_out_b, w_up_b, bu, w_down_b, bd, ln,
                    bs, sbs, sbs, sbs, n_past, False)
    y_s = y_s.reshape(bs, sbs, D)[:, :s]
    k_s = zs3[:, :, Z_K * D:(Z_K + 1) * D].astype(_F32)
    v_s = zs3[:, :, Z_V * D:(Z_V + 1) * D].astype(_F32)

    return (y_p.reshape(bp, t, D), y_s.reshape(bs, s, D),
            _heads(k_p, bp, rows), _heads(v_p, bp, rows), conv_p[None], h_p.reshape(1, bp, D),
            _heads(k_s, bs, s), _heads(v_s, bs, s), conv_s[None], h_s.reshape(1, bs, D))


def kernel(x_prompt, x_sample, c_prompt, c_sample, cache_k, cache_v, state_conv, state_lru, w_ada, b_ada, w_in, conv_w, conv_b, w_rg, b_rg, w_ig, b_ig, lru_lambda, rel_bias, w_out, ln1_g, ln1_b, w_up, b_up, w_down, b_down, ln2_g, ln2_b):
    return _forward(x_prompt, x_sample, c_prompt, c_sample, cache_k, cache_v, state_conv,
                    state_lru, w_ada, b_ada, w_in, conv_w, conv_b, w_rg, b_rg, w_ig, b_ig,
                    lru_lambda, rel_bias, w_out, ln1_g, ln1_b, w_up, b_up, w_down, b_down,
                    ln2_g, ln2_b)
```

```python
import functools
import math

import jax
import jax.numpy as jnp
from jax import lax
from jax.experimental import pallas as pl
from jax.experimental.pallas import tpu as pltpu

D = 1024
CHUNK = 64
BAND_ROWS = 8 * CHUNK
N_HEADS = 8
HEAD_DIM = D // N_HEADS
N_LRU_BLOCKS = 8
LRU_BLOCK = D // N_LRU_BLOCKS
CONV_W = 4
LRU_C = 8.0
MAX_REL = 128
ATT_SCALE = HEAD_DIM ** -0.5
NEG_INF = -1e30
D_FF = 4 * D
D_IN = 7 * D
DEPTH = 1
ALPHA = (2 * DEPTH) ** 0.25
LN_EPS = 1e-5

LOG2E = math.log2(math.e)
Q_SCALE = ATT_SCALE * LOG2E
ROLL_W = 768
SC_AHEAD = 4
SC_SLOTS = SC_AHEAD + 1
GELU_C1 = math.sqrt(2.0 / math.pi)
GELU_C2 = GELU_C1 * 0.044715

SUBLANES = 8
VMEM_LIMIT = 56 * 1024 * 1024
VMEM_LIMIT_ATTN_OUT = 62 * 1024 * 1024

COL_XR, COL_GL, COL_Q, COL_K, COL_V, COL_GA, COL_GB = range(7)
Z_SOURCE_COLS = (COL_Q, COL_K, COL_V, COL_GB)
Z_Q, Z_K, Z_V, Z_GB = range(4)
D_Z = len(Z_SOURCE_COLS) * D
PROJ_W = 256
Z_ROW_SPLIT = 2
Z_MIN_ROWS = 256
LRU_STAGES = 7
TOKEN_GROUP_PARTS = (1, 1)
TOKEN_MIN_PART_ROWS = 256
FF_CHUNK = 512

_F32 = jnp.float32
_BF16 = jnp.bfloat16
_DONE = object()


def _resident(shape):
    n = len(shape)
    return pl.BlockSpec(shape, lambda *_: (0,) * n, pipeline_mode=pl.Buffered(1))


def _sigmoid(v):
    return 0.5 * jnp.tanh(0.5 * v) + 0.5


def _layer_norm(v, g, b):
    mu = jnp.mean(v, axis=-1, keepdims=True)
    vc = v - mu
    var = jnp.mean(vc * vc, axis=-1, keepdims=True)
    return vc * lax.rsqrt(var + LN_EPS) * g + b


def _ada_kernel(c_ref, w_ref, b_ref, o_ref):
    c = c_ref[...]
    s = (c * _sigmoid(c)).astype(_BF16)
    o_ref[...] = jnp.dot(s, w_ref[...], preferred_element_type=_F32) + b_ref[...]


def _ada(c, w_ada, b_ada):
    n = c.shape[0]
    return pl.pallas_call(
        _ada_kernel,
        out_shape=jax.ShapeDtypeStruct((n, 6 * D), _F32),
        compiler_params=pltpu.CompilerParams(vmem_limit_bytes=VMEM_LIMIT),
        name="ada",
    )(c, w_ada, b_ada)


def _seg_pitch(seg):
    return seg if (seg // SUBLANES) % 2 == 1 else seg + SUBLANES


def _mix_in_kernel(x_ref, mod_ref, w_ref, hist_ref, h0_ref, cw_ref, cb_ref, wg_ref, bg_ref,
                   lam_ref, z_ref, ya_ref, conv_ref, hlast_ref, kf_ref, vf_ref, stage, gl_scr,
                   ga_scr, hist_scr, hc_scr, *, tt, reset_first):
    i = pl.program_id(1)
    seg = tt // SUBLANES
    pitch = _seg_pitch(seg)
    n_hist = CONV_W - 1

    @pl.when(i == 0)
    def _():
        hist_scr[...] = hist_ref[...]
        hc_scr[...] = h0_ref[...]

    lam = lam_ref[...]
    half_coef = (-0.5 * LRU_C * LOG2E) * (jnp.maximum(-lam, 0.0)
                                         + jnp.log1p(jnp.exp(-jnp.abs(lam))))
    srow = lax.broadcasted_iota(jnp.int32, (SUBLANES, LRU_BLOCK), 0)

    sh1 = mod_ref[:, 0:D]
    sc1 = mod_ref[:, D:2 * D]
    u = (x_ref[...] * (1.0 + sc1) + sh1).astype(_BF16)

    def project(c0):
        return jnp.dot(u, w_ref[:, c0:c0 + PROJ_W], preferred_element_type=_F32)

    def project_rnn(k):
        c0 = k * PROJ_W
        xr = project(COL_XR * D + c0)
        conv_ref[:, c0:c0 + PROJ_W] = xr[tt - n_hist:tt, :]
        for m in range(PROJ_W // LRU_BLOCK):
            for s in range(SUBLANES):
                stage[k * (PROJ_W // LRU_BLOCK) + m, s * pitch:s * pitch + seg, :] = (
                    xr[s * seg:(s + 1) * seg, m * LRU_BLOCK:(m + 1) * LRU_BLOCK])
        gl_scr[:, c0:c0 + PROJ_W] = project(COL_GL * D + c0)
        ga_scr[:, c0:c0 + PROJ_W] = project(COL_GA * D + c0)

    z_rows = tt // Z_ROW_SPLIT if tt >= Z_ROW_SPLIT * Z_MIN_ROWS else tt

    def project_z(k):
        kc, kr = divmod(k, tt // z_rows)
        col, off = divmod(kc * PROJ_W, D)
        c0 = Z_SOURCE_COLS[col] * D + off
        rows = slice(kr * z_rows, (kr + 1) * z_rows)
        zk = jnp.dot(u[rows], w_ref[:, c0:c0 + PROJ_W], preferred_element_type=_F32)
        if Z_SOURCE_COLS[col] == COL_Q:
            zk = zk * Q_SCALE
        elif Z_SOURCE_COLS[col] == COL_K:
            kf_ref[rows, off:off + PROJ_W] = zk
        elif Z_SOURCE_COLS[col] == COL_V:
            vf_ref[rows, off:off + PROJ_W] = zk
        elif Z_SOURCE_COLS[col] == COL_GB:
            zk = 0.5 * jnp.tanh(zk) + 0.5
        z_ref[rows, kc * PROJ_W:(kc + 1) * PROJ_W] = zk.astype(_BF16)

    def step_regs(v):
        return [v[g * SUBLANES:(g + 1) * SUBLANES] for g in range(seg)]

    def lru_block(n):
        cols = slice(n * LRU_BLOCK, (n + 1) * LRU_BLOCK)
        xg = [stage[n, pl.ds(g, SUBLANES, stride=pitch), :] for g in range(seg)]
        xt = jnp.concatenate(xg, axis=0)

        pre = []
        for j in range(1, n_hist + 1):
            hrow = jnp.broadcast_to(hist_scr[n_hist - j:n_hist - j + 1, cols], (SUBLANES, LRU_BLOCK))
            pre.append(jnp.where(srow == 0, hrow, pltpu.roll(xg[seg - j], 1, 0)))

        def back(j):
            head = [pre[k - 1] for k in range(j, 0, -1)]
            return jnp.concatenate(head + [xt[:tt - j * SUBLANES]], axis=0)

        xc = cb_ref[:, cols] + back(n_hist) * cw_ref[0:1, cols]
        for j in range(1, n_hist):
            xc = xc + back(n_hist - j) * cw_ref[j:j + 1, cols]
        xc = xc + xt * cw_ref[n_hist:CONV_W, cols]

        yield
        gates = jnp.dot(xc.astype(_BF16), wg_ref[n], preferred_element_type=_F32) + bg_ref[n]
        coef = half_coef[:, cols]
        a = jnp.exp2(coef * jnp.tanh(gates[:, :LRU_BLOCK]) + coef)
        mult = jnp.exp2(0.5 * jnp.log2(1.0 - a * a))
        half_xc = 0.5 * xc
        ix = half_xc * jnp.tanh(gates[:, LRU_BLOCK:]) + half_xc
        av = step_regs(a)
        bv = step_regs(mult * ix)
        if reset_first:
            first = jnp.logical_and(srow == 0, i == 0)
            av[0] = jnp.where(first, 0.0, av[0])
            bv[0] = jnp.where(first, ix[:SUBLANES], bv[0])

        yield
        seg_a, seg_b = av[0], bv[0]
        for g in range(1, seg):
            seg_b = av[g] * seg_b + bv[g]
            seg_a = av[g] * seg_a
            if g == seg // 2:
                yield
        for d in (1, 2, 4):
            keep = srow >= d
            a_sh = jnp.where(keep, pltpu.roll(seg_a, d, 0), 1.0)
            b_sh = jnp.where(keep, pltpu.roll(seg_b, d, 0), 0.0)
            seg_b = seg_a * b_sh + seg_b
            seg_a = seg_a * a_sh
        yield
        h_in = jnp.broadcast_to(hc_scr[:, cols], (SUBLANES, LRU_BLOCK))
        h_end = seg_a * h_in + seg_b
        h = jnp.where(srow == 0, h_in, pltpu.roll(h_end, 1, 0))
        for g in range(seg):
            h = av[g] * h + bv[g]
            stage[n, pl.ds(g, SUBLANES, stride=pitch), :] = h
            if g == seg // 2:
                yield
        hc_scr[:, cols] = h[SUBLANES - 1:SUBLANES]
        hlast_ref[:, cols] = h[SUBLANES - 1:SUBLANES]

        yield
        for s in range(SUBLANES):
            rows = slice(s * seg, (s + 1) * seg)
            gl = gl_scr[rows, cols]
            hq = stage[n, s * pitch:s * pitch + seg, :] * (0.25 * gl)
            w = hq + hq * jnp.tanh(gl * (GELU_C1 + GELU_C2 * (gl * gl)))
            ya_ref[rows, cols] = (w + w * jnp.tanh(ga_scr[rows, cols])).astype(_BF16)
            if s == SUBLANES // 2 - 1:
                yield

    blocks_per_chunk = PROJ_W // LRU_BLOCK
    n_z = (D_Z // PROJ_W) * (tt // z_rows)
    n_slots = N_LRU_BLOCKS * LRU_STAGES
    slot = 0
    project_rnn(0)
    for n in range(N_LRU_BLOCKS):
        if n % blocks_per_chunk == 0 and n + blocks_per_chunk < N_LRU_BLOCKS:
            project_rnn(n // blocks_per_chunk + 1)
        for _ in lru_block(n):
            for k in range(slot * n_z // n_slots, (slot + 1) * n_z // n_slots):
                project_z(k)
            slot += 1
    assert slot == n_slots

    hist_scr[...] = conv_ref[...]


def _mix_in(x2, mod3, w_in, hist, h0, conv_w, conv_b, wg, bg, lam, bsz, t, tt, reset_first):
    nt = t // tt
    kern = functools.partial(_mix_in_kernel, tt=tt, reset_first=reset_first)
    return pl.pallas_call(
        kern,
        grid=(bsz, nt),
        in_specs=[
            pl.BlockSpec((tt, D), lambda b, i: (b * nt + i, 0)),
            pl.BlockSpec((None, 1, 6 * D), lambda b, i: (b, 0, 0)),
            _resident((D, D_IN)),
            pl.BlockSpec((None, CONV_W - 1, D), lambda b, i: (b, 0, 0)),
            pl.BlockSpec((None, 1, D), lambda b, i: (b, 0, 0)),
            _resident((CONV_W, D)),
            _resident((1, D)),
            _resident((N_LRU_BLOCKS, LRU_BLOCK, 2 * LRU_BLOCK)),
            _resident((N_LRU_BLOCKS, 1, 2 * LRU_BLOCK)),
            _resident((1, D)),
        ],
        out_specs=[
            pl.BlockSpec((tt, D_Z), lambda b, i: (b * nt + i, 0)),
            pl.BlockSpec((tt, D), lambda b, i: (b * nt + i, 0)),
            pl.BlockSpec((None, CONV_W - 1, D), lambda b, i: (b, 0, 0)),
            pl.BlockSpec((None, 1, D), lambda b, i: (b, 0, 0)),
            pl.BlockSpec((tt, D), lambda b, i: (b, 0)),
            pl.BlockSpec((tt, D), lambda b, i: (b, 0)),
        ],
        out_shape=[
            jax.ShapeDtypeStruct((bsz * t, D_Z), _BF16),
            jax.ShapeDtypeStruct((bsz * t, D), _BF16),
            jax.ShapeDtypeStruct((bsz, CONV_W - 1, D), _F32),
            jax.ShapeDtypeStruct((bsz, 1, D), _F32),
            jax.ShapeDtypeStruct((bsz * tt, D), _F32),
            jax.ShapeDtypeStruct((bsz * tt, D), _F32),
        ],
        scratch_shapes=[
            pltpu.VMEM((N_LRU_BLOCKS, SUBLANES * _seg_pitch(tt // SUBLANES), LRU_BLOCK), _F32),
            pltpu.VMEM((tt, D), _F32),
            pltpu.VMEM((tt, D), _F32),
            pltpu.VMEM((CONV_W - 1, D), _F32),
            pltpu.VMEM((1, D), _F32),
        ],
        compiler_params=pltpu.CompilerParams(
            dimension_semantics=("arbitrary", "arbitrary"), vmem_limit_bytes=VMEM_LIMIT),
        name="mix_in",
    )(x2, mod3, w_in, hist, h0, conv_w, conv_b, wg, bg, lam)


def _attn_out_kernel(q_ref, kp_ref, kc_ref, vp_ref, vc_ref, sgb_ref, row0_ref,
                     x_ref, ya_ref, mod_ref, wo_ref, wu_ref, bu_ref, wd_ref, bd_ref, ln_ref,
                     y_ref, bias_scr, sc_scr, yb_scr, *, tm, sb, band, nt):
    j = pl.program_id(0)
    i = jnp.minimum(j, pl.num_programs(0) - 2) % nt
    past = kp_ref.shape[0]
    win = past + sb
    nsub = tm // sb
    slot_w = j % 2
    slot_r = 1 - slot_w

    @pl.when(j == 0)
    def _first_step():
        yb_scr[...] = jnp.zeros_like(yb_scr)
        qq = lax.broadcasted_iota(jnp.int32, (sb, win), 0)
        kk = lax.broadcasted_iota(jnp.int32, (sb, win), 1)
        if band:
            dc = (kk // CHUNK) - (qq // CHUNK)
            ok = jnp.logical_and(dc >= 0, dc <= BAND_ROWS // CHUNK)
        else:
            ok = kk < past + CHUNK
        for h in range(N_HEADS):
            base = jnp.broadcast_to(row0_ref[h:h + 1, :], (sb, ROLL_W))
            toep = pltpu.roll(base, 0, 1, stride=1, stride_axis=0)[:, :win] * LOG2E
            bias_scr[h] = jnp.where(ok, toep, NEG_INF)
        bias_scr[N_HEADS] = jnp.full((sb, win), NEG_INF, _F32)

    def rows(prev_ref, cur_ref, so, cols):
        parts = []
        if so < past:
            parts.append(prev_ref[so:past, cols])
        parts.append(cur_ref[max(so - past, 0):so + sb, cols])
        return parts[0] if len(parts) == 1 else jnp.concatenate(parts, axis=0)

    def scores(n, s, h):
        cols = slice(h * HEAD_DIM, (h + 1) * HEAD_DIM)
        qh = q_ref[s * sb:(s + 1) * sb, cols]
        kh = rows(kp_ref, kc_ref, s * sb, cols)
        sc = lax.dot_general(qh, kh, (((1,), (1,)), ((), ())), preferred_element_type=_F32)
        before_start = past - s * sb if band else 0
        biased = []
        for c0 in range(0, win, HEAD_DIM):
            lanes = slice(c0, c0 + HEAD_DIM)
            slot = jnp.where(i == 0, N_HEADS, h) if c0 + HEAD_DIM <= before_start else h
            biased.append(sc[:, lanes] + bias_scr[slot, :, lanes])
        sc_scr[n % SC_SLOTS] = jnp.concatenate(biased, axis=1)

    def probs(n, s, h):
        sc = sc_scr[n % SC_SLOTS]
        return jnp.exp2(sc - jnp.max(sc, axis=-1, keepdims=True)).astype(_BF16)

    ones_cols = jnp.ones((win, HEAD_DIM), _BF16)

    def weighted_values(s, h, p):
        cols = slice(h * HEAD_DIM, (h + 1) * HEAD_DIM)
        vh = jnp.concatenate([rows(vp_ref, vc_ref, s * sb, cols), ones_cols], axis=1)
        ol = jnp.dot(p, vh, preferred_element_type=_F32)
        gate = sgb_ref[s * sb:(s + 1) * sb, cols].astype(_F32)
        scale = gate * (1.0 / ol[:, HEAD_DIM:])
        yb_scr[slot_w, s * sb:(s + 1) * sb, cols] = (ol[:, :HEAD_DIM] * scale).astype(_BF16)

    def attention_stages():
        pairs = [(s, h) for s in range(nsub) for h in range(N_HEADS)]
        for n in range(min(SC_AHEAD, len(pairs))):
            scores(n, *pairs[n])
        ready = probs(0, *pairs[0])
        for n, (s, h) in enumerate(pairs):
            yield
            if n + SC_AHEAD < len(pairs):
                scores(n + SC_AHEAD, *pairs[n + SC_AHEAD])
            current = ready
            if n + 1 < len(pairs):
                ready = probs(n + 1, *pairs[n + 1])
            weighted_values(s, h, current)

    g1 = mod_ref[:, 2 * D:3 * D]
    sh2 = mod_ref[:, 3 * D:4 * D]
    sc2 = mod_ref[:, 4 * D:5 * D]
    g2 = mod_ref[:, 5 * D:6 * D]

    def token_stages(rows):
        yb = yb_scr[slot_r, rows, :]
        merged = (ya_ref[rows, :].astype(_F32) + yb.astype(_F32)).astype(_BF16)
        yield
        o = jnp.dot(merged, wo_ref[...], preferred_element_type=_F32)
        yield
        x1 = _layer_norm(ALPHA * x_ref[rows, :] + (1.0 + g1) * o, ln_ref[0:1, :], ln_ref[1:2, :])
        u2 = (x1 * (1.0 + sc2) + sh2).astype(_BF16)
        yield
        f = bd_ref[...]
        for c in range(D_FF // FF_CHUNK):
            cols = slice(c * FF_CHUNK, (c + 1) * FF_CHUNK)
            hid = jnp.dot(u2, wu_ref[:, cols], preferred_element_type=_F32) + bu_ref[:, cols]
            hid = jnp.square(jnp.maximum(hid, 0.0)).astype(_BF16)
            yield
            f = f + jnp.dot(hid, wd_ref[cols, :], preferred_element_type=_F32)
            yield
        y_ref[rows, :] = _layer_norm(ALPHA * x1 + (1.0 + g2) * f, ln_ref[2:3, :], ln_ref[3:4, :])

    def staggered(gens):
        live = list(range(len(gens)))
        tick = 0
        while True:
            for k in list(live):
                if tick >= k and next(gens[k], _DONE) is _DONE:
                    live.remove(k)
            tick += 1
            if not live:
                return
            yield

    parts = TOKEN_GROUP_PARTS if tm >= sum(TOKEN_GROUP_PARTS) * TOKEN_MIN_PART_ROWS else (1,)
    unit = tm // sum(parts)
    bounds = [unit * sum(parts[:k]) for k in range(len(parts) + 1)]
    tokens = staggered([token_stages(slice(bounds[k], bounds[k + 1]))
                        for k in range(len(parts))])
    attention = attention_stages()
    n_att = nsub * N_HEADS + 1
    n_tok = len(parts) - 1 + 4 + 2 * (D_FF // FF_CHUNK)
    done_att = done_tok = 0
    while done_att < n_att or done_tok < n_tok:
        if done_tok == n_tok or (done_att < n_att and done_att * n_tok <= done_tok * n_att):
            next(attention, _DONE)
            done_att += 1
        else:
            next(tokens, _DONE)
            done_tok += 1
    assert next(attention, _DONE) is _DONE and next(tokens, _DONE) is _DONE


def _attn_out(z, kp_arr, vp_arr, sample_cache, row0, x2, ya, mod3, w_out, w_up, b_up, w_down,
              b_down, ln, bsz, t, tm, sb, past, band):
    nt = t // tm
    n_tiles = bsz * nt
    win = past + sb
    kern = functools.partial(_attn_out_kernel, tm=tm, sb=sb, band=band, nt=nt)

    def att_tile(j):
        return jnp.minimum(j, n_tiles - 1)

    def tok_tile(j):
        return jnp.maximum(j - 1, 0)

    def prev_rows(col):
        if sample_cache:
            return lambda j: (att_tile(j), 0)
        return lambda j: (jnp.where(att_tile(j) % nt == 0, att_tile(j), att_tile(j) - 1), col)

    return pl.pallas_call(
        kern,
        grid=(n_tiles + 1,),
        in_specs=[
            pl.BlockSpec((tm, D), lambda j: (att_tile(j), Z_Q)),
            pl.BlockSpec((past, D), prev_rows(Z_K)),
            pl.BlockSpec((tm, D), lambda j: (att_tile(j), Z_K)),
            pl.BlockSpec((past, D), prev_rows(Z_V)),
            pl.BlockSpec((tm, D), lambda j: (att_tile(j), Z_V)),
            pl.BlockSpec((tm, D), lambda j: (att_tile(j), Z_GB)),
            _resident((N_HEADS, ROLL_W)),
            pl.BlockSpec((tm, D), lambda j: (tok_tile(j), 0)),
            pl.BlockSpec((tm, D), lambda j: (tok_tile(j), 0)),
            pl.BlockSpec((None, 1, 6 * D), lambda j: (tok_tile(j) // nt, 0, 0)),
            _resident((D, D)),
            _resident((D, D_FF)),
            _resident((1, D_FF)),
            _resident((D_FF, D)),
            _resident((1, D)),
            _resident((4, D)),
        ],
        out_specs=pl.BlockSpec((tm, D), lambda j: (tok_tile(j), 0)),
        out_shape=jax.ShapeDtypeStruct((bsz * t, D), _F32),
        scratch_shapes=[
            pltpu.VMEM((N_HEADS + 1, sb, win), _F32),
            pltpu.VMEM((SC_SLOTS, sb, win), _F32),
            pltpu.VMEM((2, tm, D), _BF16),
        ],
        compiler_params=pltpu.CompilerParams(
            dimension_semantics=("arbitrary",), vmem_limit_bytes=VMEM_LIMIT_ATTN_OUT),
        name="attn_out",
    )(z, kp_arr, z, vp_arr, z, z, row0, x2, ya, mod3, w_out, w_up, b_up, w_down, b_down, ln)


def _rel_bias_row0(table, past):
    far = jnp.broadcast_to(table[:, 2 * MAX_REL:], (N_HEADS, past - MAX_REL + 1))
    near = table[:, 2 * MAX_REL - 1::-1]
    tail = jnp.broadcast_to(table[:, 2 * MAX_REL:],
                            (N_HEADS, ROLL_W - (past + MAX_REL + 1)))
    return jnp.concatenate([far, near, tail], axis=1).astype(_F32)


def _heads(a, bsz, rows):
    return a.reshape(1, bsz, rows, N_HEADS, HEAD_DIM)


@jax.jit
def _forward(x_prompt, x_sample, c_prompt, c_sample, cache_k, cache_v, state_conv, state_lru,
             w_ada, b_ada, w_in, conv_w, conv_b, w_rg, b_rg, w_ig, b_ig, lru_lambda, rel_bias,
             w_out, ln1_g, ln1_b, w_up, b_up, w_down, b_down, ln2_g, ln2_b):
    bp, t, _ = x_prompt.shape
    bs, s, _ = x_sample.shape
    l = 0
    col_scale = jnp.where(jnp.arange(D_IN) >= COL_GA * D, 0.5, 1.0).astype(_F32)
    w_in_b = (w_in[l] * col_scale[None, :]).astype(_BF16)
    w_out_b = w_out[l].astype(_BF16)
    w_up_b = w_up[l].astype(_BF16)
    w_down_b = w_down[l].astype(_BF16)
    wg = (0.5 * jnp.concatenate([w_rg[l], w_ig[l]], axis=-1)).astype(_BF16)
    bg = 0.5 * jnp.concatenate([b_rg[l], b_ig[l]], axis=-1)[:, None, :]
    lam = lru_lambda[l][None, :]
    cb = conv_b[l][None, :]
    ln = jnp.stack([ln1_g[l], ln1_b[l], ln2_g[l], ln2_b[l]])
    bu = b_up[l][None, :]
    bd = b_down[l][None, :]

    mod = _ada(jnp.concatenate([c_prompt, c_sample], axis=0), w_ada[l].astype(_BF16),
               b_ada[l][None, :])
    mod_p = mod[:bp, None, :]
    mod_s = mod[bp:, None, :]

    tm = 512
    xp2 = x_prompt.reshape(bp * t, D)
    z, ya, conv_p, h_p, k_p, v_p = _mix_in(xp2, mod_p, w_in_b, jnp.zeros((bp, CONV_W - 1, D), _F32),
                                 jnp.zeros((bp, 1, D), _F32), conv_w[l], cb, wg, bg, lam,
                                 bp, t, tm, True)
    y_p = _attn_out(z, z, z, False, _rel_bias_row0(rel_bias[l], BAND_ROWS), xp2, ya, mod_p,
                    w_out_b, w_up_b, bu, w_down_b, bd, ln, bp, t, tm, 2 * CHUNK, BAND_ROWS, True)
    rows = min(BAND_ROWS, t)
    assert rows == tm

    xs2 = x_sample.reshape(bs * s, D)
    zs, ya_s, conv_s, h_s, k_s, v_s = _mix_in(xs2, mod_s, w_in_b, state_conv[l], state_lru[l][:, None, :],
                                    conv_w[l], cb, wg, bg, lam, bs, s, s, False)
    sbs = 2 * CHUNK

    def pad_rows(a):
        a3 = a.reshape(bs, s, a.shape[-1])
        return jnp.pad(a3, ((0, 0), (0, sbs - s), (0, 0))).reshape(bs * sbs, a.shape[-1])

    n_past = cache_k.shape[2]
    ck = cache_k[l].reshape(bs * n_past, D).astype(_BF16)
    cv = cache_v[l].reshape(bs * n_past, D).astype(_BF16)
    y_s = _attn_out(pad_rows(zs), ck, cv, True, _rel_bias_row0(rel_bias[l], n_past),
                    pad_rows(xs2), pad_rows(ya_s), mod_s, w_out_b, w_up_b, bu, w_down_b, bd, ln,
                    bs, sbs, sbs, sbs, n_past, False)
    y_s = y_s.reshape(bs, sbs, D)[:, :s]

    return (y_p.reshape(bp, t, D), y_s.reshape(bs, s, D),
            _heads(k_p, bp, rows), _heads(v_p, bp, rows), conv_p[None], h_p.reshape(1, bp, D),
            _heads(k_s, bs, s), _heads(v_s, bs, s), conv_s[None], h_s.reshape(1, bs, D))


def kernel(x_prompt, x_sample, c_prompt, c_sample, cache_k, cache_v, state_conv, state_lru, w_ada, b_ada, w_in, conv_w, conv_b, w_rg, b_rg, w_ig, b_ig, lru_lambda, rel_bias, w_out, ln1_g, ln1_b, w_up, b_up, w_down, b_down, ln2_g, ln2_b):
    return _forward(x_prompt, x_sample, c_prompt, c_sample, cache_k, cache_v, state_conv,
                    state_lru, w_ada, b_ada, w_in, conv_w, conv_b, w_rg, b_rg, w_ig, b_ig,
                    lru_lambda, rel_bias, w_out, ln1_g, ln1_b, w_up, b_up, w_down, b_down,
                    ln2_g, ln2_b)
```

```python
import functools
import math

import jax
import jax.numpy as jnp
from jax import lax
from jax.experimental import pallas as pl
from jax.experimental.pallas import tpu as pltpu

D = 1024
CHUNK = 64
BAND_ROWS = 8 * CHUNK
N_HEADS = 8
HEAD_DIM = D // N_HEADS
N_LRU_BLOCKS = 8
LRU_BLOCK = D // N_LRU_BLOCKS
CONV_W = 4
LRU_C = 8.0
MAX_REL = 128
ATT_SCALE = HEAD_DIM ** -0.5
NEG_INF = -1e30
D_FF = 4 * D
D_IN = 7 * D
DEPTH = 1
ALPHA = (2 * DEPTH) ** 0.25
LN_EPS = 1e-5

LOG2E = math.log2(math.e)
Q_SCALE = ATT_SCALE * LOG2E
ROLL_W = 768
SC_AHEAD = 4
SC_SLOTS = SC_AHEAD + 1
GELU_C1 = math.sqrt(2.0 / math.pi)
GELU_C2 = GELU_C1 * 0.044715

SUBLANES = 8
VMEM_LIMIT = 56 * 1024 * 1024
VMEM_LIMIT_ATTN_OUT = 62 * 1024 * 1024

COL_XR, COL_GL, COL_Q, COL_K, COL_V, COL_GA, COL_GB = range(7)
Z_SOURCE_COLS = (COL_Q, COL_K, COL_V, COL_GB)
Z_Q, Z_K, Z_V, Z_GB = range(4)
D_Z = len(Z_SOURCE_COLS) * D
PROJ_W = 256
Z_ROW_SPLIT = 2
Z_MIN_ROWS = 256
LRU_STAGES = 7
TOKEN_GROUP_PARTS = (1, 1)
TOKEN_MIN_PART_ROWS = 256
FF_CHUNK = 512

_F32 = jnp.float32
_BF16 = jnp.bfloat16
_DONE = object()


def _resident(shape):
    n = len(shape)
    return pl.BlockSpec(shape, lambda *_: (0,) * n, pipeline_mode=pl.Buffered(1))


def _sigmoid(v):
    return 0.5 * jnp.tanh(0.5 * v) + 0.5


def _layer_norm(v, g, b):
    mu = jnp.mean(v, axis=-1, keepdims=True)
    vc = v - mu
    var = jnp.mean(vc * vc, axis=-1, keepdims=True)
    return vc * lax.rsqrt(var + LN_EPS) * g + b


def _ada_kernel(c_ref, w_ref, b_ref, o_ref):
    c = c_ref[...]
    s = (c * _sigmoid(c)).astype(_BF16)
    o_ref[...] = jnp.dot(s, w_ref[...], preferred_element_type=_F32) + b_ref[...]


def _ada(c, w_ada, b_ada):
    n = c.shape[0]
    return pl.pallas_call(
        _ada_kernel,
        out_shape=jax.ShapeDtypeStruct((n, 6 * D), _F32),
        compiler_params=pltpu.CompilerParams(vmem_limit_bytes=VMEM_LIMIT),
        name="ada",
    )(c, w_ada, b_ada)


def _seg_pitch(seg):
    return seg if (seg // SUBLANES) % 2 == 1 else seg + SUBLANES


def _mix_in_kernel(x_ref, mod_ref, w_ref, hist_ref, h0_ref, cw_ref, cb_ref, wg_ref, bg_ref,
                   lam_ref, z_ref, ya_ref, conv_ref, hlast_ref, stage, gl_scr, ga_scr, hist_scr,
                   hc_scr, *, tt, reset_first):
    i = pl.program_id(1)
    seg = tt // SUBLANES
    pitch = _seg_pitch(seg)
    n_hist = CONV_W - 1

    @pl.when(i == 0)
    def _():
        hist_scr[...] = hist_ref[...]
        hc_scr[...] = h0_ref[...]

    lam = lam_ref[...]
    half_coef = (-0.5 * LRU_C * LOG2E) * (jnp.maximum(-lam, 0.0)
                                         + jnp.log1p(jnp.exp(-jnp.abs(lam))))
    srow = lax.broadcasted_iota(jnp.int32, (SUBLANES, LRU_BLOCK), 0)

    sh1 = mod_ref[:, 0:D]
    sc1 = mod_ref[:, D:2 * D]
    u = (x_ref[...] * (1.0 + sc1) + sh1).astype(_BF16)

    def project(c0):
        return jnp.dot(u, w_ref[:, c0:c0 + PROJ_W], preferred_element_type=_F32)

    def project_rnn(k):
        c0 = k * PROJ_W
        xr = project(COL_XR * D + c0)
        conv_ref[:, c0:c0 + PROJ_W] = xr[tt - n_hist:tt, :]
        for m in range(PROJ_W // LRU_BLOCK):
            for s in range(SUBLANES):
                stage[k * (PROJ_W // LRU_BLOCK) + m, s * pitch:s * pitch + seg, :] = (
                    xr[s * seg:(s + 1) * seg, m * LRU_BLOCK:(m + 1) * LRU_BLOCK])
        gl_scr[:, c0:c0 + PROJ_W] = project(COL_GL * D + c0)
        ga_scr[:, c0:c0 + PROJ_W] = project(COL_GA * D + c0)

    z_rows = tt // Z_ROW_SPLIT if tt >= Z_ROW_SPLIT * Z_MIN_ROWS else tt

    def project_z(k):
        kc, kr = divmod(k, tt // z_rows)
        col, off = divmod(kc * PROJ_W, D)
        c0 = Z_SOURCE_COLS[col] * D + off
        rows = slice(kr * z_rows, (kr + 1) * z_rows)
        zk = jnp.dot(u[rows], w_ref[:, c0:c0 + PROJ_W], preferred_element_type=_F32)
        if Z_SOURCE_COLS[col] == COL_Q:
            zk = zk * Q_SCALE
        elif Z_SOURCE_COLS[col] == COL_GB:
            zk = 0.5 * jnp.tanh(zk) + 0.5
        z_ref[rows, kc * PROJ_W:(kc + 1) * PROJ_W] = zk.astype(_BF16)

    def step_regs(v):
        return [v[g * SUBLANES:(g + 1) * SUBLANES] for g in range(seg)]

    def lru_block(n):
        cols = slice(n * LRU_BLOCK, (n + 1) * LRU_BLOCK)
        xg = [stage[n, pl.ds(g, SUBLANES, stride=pitch), :] for g in range(seg)]
        xt = jnp.concatenate(xg, axis=0)

        pre = []
        for j in range(1, n_hist + 1):
            hrow = jnp.broadcast_to(hist_scr[n_hist - j:n_hist - j + 1, cols], (SUBLANES, LRU_BLOCK))
            pre.append(jnp.where(srow == 0, hrow, pltpu.roll(xg[seg - j], 1, 0)))

        def back(j):
            head = [pre[k - 1] for k in range(j, 0, -1)]
            return jnp.concatenate(head + [xt[:tt - j * SUBLANES]], axis=0)

        xc = cb_ref[:, cols] + back(n_hist) * cw_ref[0:1, cols]
        for j in range(1, n_hist):
            xc = xc + back(n_hist - j) * cw_ref[j:j + 1, cols]
        xc = xc + xt * cw_ref[n_hist:CONV_W, cols]

        yield
        gates = jnp.dot(xc.astype(_BF16), wg_ref[n], preferred_element_type=_F32) + bg_ref[n]
        coef = half_coef[:, cols]
        a = jnp.exp2(coef * jnp.tanh(gates[:, :LRU_BLOCK]) + coef)
        mult = jnp.exp2(0.5 * jnp.log2(1.0 - a * a))
        half_xc = 0.5 * xc
        ix = half_xc * jnp.tanh(gates[:, LRU_BLOCK:]) + half_xc
        av = step_regs(a)
        bv = step_regs(mult * ix)
        if reset_first:
            first = jnp.logical_and(srow == 0, i == 0)
            av[0] = jnp.where(first, 0.0, av[0])
            bv[0] = jnp.where(first, ix[:SUBLANES], bv[0])

        yield
        seg_a, seg_b = av[0], bv[0]
        for g in range(1, seg):
            seg_b = av[g] * seg_b + bv[g]
            seg_a = av[g] * seg_a
            if g == seg // 2:
                yield
        for d in (1, 2, 4):
            keep = srow >= d
            a_sh = jnp.where(keep, pltpu.roll(seg_a, d, 0), 1.0)
            b_sh = jnp.where(keep, pltpu.roll(seg_b, d, 0), 0.0)
            seg_b = seg_a * b_sh + seg_b
            seg_a = seg_a * a_sh
        yield
        h_in = jnp.broadcast_to(hc_scr[:, cols], (SUBLANES, LRU_BLOCK))
        h_end = seg_a * h_in + seg_b
        h = jnp.where(srow == 0, h_in, pltpu.roll(h_end, 1, 0))
        for g in range(seg):
            h = av[g] * h + bv[g]
            stage[n, pl.ds(g, SUBLANES, stride=pitch), :] = h
            if g == seg // 2:
                yield
        hc_scr[:, cols] = h[SUBLANES - 1:SUBLANES]
        hlast_ref[:, cols] = h[SUBLANES - 1:SUBLANES]

        yield
        for s in range(SUBLANES):
            rows = slice(s * seg, (s + 1) * seg)
            gl = gl_scr[rows, cols]
            hq = stage[n, s * pitch:s * pitch + seg, :] * (0.25 * gl)
            w = hq + hq * jnp.tanh(gl * (GELU_C1 + GELU_C2 * (gl * gl)))
            ya_ref[rows, cols] = (w + w * jnp.tanh(ga_scr[rows, cols])).astype(_BF16)
            if s == SUBLANES // 2 - 1:
                yield

    blocks_per_chunk = PROJ_W // LRU_BLOCK
    n_z = (D_Z // PROJ_W) * (tt // z_rows)
    n_slots = N_LRU_BLOCKS * LRU_STAGES
    slot = 0
    project_rnn(0)
    for n in range(N_LRU_BLOCKS):
        if n % blocks_per_chunk == 0 and n + blocks_per_chunk < N_LRU_BLOCKS:
            project_rnn(n // blocks_per_chunk + 1)
        for _ in lru_block(n):
            for k in range(slot * n_z // n_slots, (slot + 1) * n_z // n_slots):
                project_z(k)
            slot += 1
    assert slot == n_slots

    hist_scr[...] = conv_ref[...]


def _mix_in(x2, mod3, w_in, hist, h0, conv_w, conv_b, wg, bg, lam, bsz, t, tt, reset_first):
    nt = t // tt
    kern = functools.partial(_mix_in_kernel, tt=tt, reset_first=reset_first)
    return pl.pallas_call(
        kern,
        grid=(bsz, nt),
        in_specs=[
            pl.BlockSpec((tt, D), lambda b, i: (b * nt + i, 0)),
            pl.BlockSpec((None, 1, 6 * D), lambda b, i: (b, 0, 0)),
            _resident((D, D_IN)),
            pl.BlockSpec((None, CONV_W - 1, D), lambda b, i: (b, 0, 0)),
            pl.BlockSpec((None, 1, D), lambda b, i: (b, 0, 0)),
            _resident((CONV_W, D)),
            _resident((1, D)),
            _resident((N_LRU_BLOCKS, LRU_BLOCK, 2 * LRU_BLOCK)),
            _resident((N_LRU_BLOCKS, 1, 2 * LRU_BLOCK)),
            _resident((1, D)),
        ],
        out_specs=[
            pl.BlockSpec((tt, D_Z), lambda b, i: (b * nt + i, 0)),
            pl.BlockSpec((tt, D), lambda b, i: (b * nt + i, 0)),
            pl.BlockSpec((None, CONV_W - 1, D), lambda b, i: (b, 0, 0)),
            pl.BlockSpec((None, 1, D), lambda b, i: (b, 0, 0)),
        ],
        out_shape=[
            jax.ShapeDtypeStruct((bsz * t, D_Z), _BF16),
            jax.ShapeDtypeStruct((bsz * t, D), _BF16),
            jax.ShapeDtypeStruct((bsz, CONV_W - 1, D), _F32),
            jax.ShapeDtypeStruct((bsz, 1, D), _F32),
        ],
        scratch_shapes=[
            pltpu.VMEM((N_LRU_BLOCKS, SUBLANES * _seg_pitch(tt // SUBLANES), LRU_BLOCK), _F32),
            pltpu.VMEM((tt, D), _F32),
            pltpu.VMEM((tt, D), _F32),
            pltpu.VMEM((CONV_W - 1, D), _F32),
            pltpu.VMEM((1, D), _F32),
        ],
        compiler_params=pltpu.CompilerParams(
            dimension_semantics=("arbitrary", "arbitrary"), vmem_limit_bytes=VMEM_LIMIT),
        name="mix_in",
    )(x2, mod3, w_in, hist, h0, conv_w, conv_b, wg, bg, lam)


def _attn_out_kernel(q_ref, kp_ref, kc_ref, vp_ref, vc_ref, sgb_ref, row0_ref,
                     x_ref, ya_ref, mod_ref, wo_ref, wu_ref, bu_ref, wd_ref, bd_ref, ln_ref,
                     y_ref, bias_scr, sc_scr, yb_scr, *, tm, sb, band, nt):
    j = pl.program_id(0)
    i = jnp.minimum(j, pl.num_programs(0) - 2) % nt
    past = kp_ref.shape[0]
    win = past + sb
    nsub = tm // sb
    slot_w = j % 2
    slot_r = 1 - slot_w

    @pl.when(j == 0)
    def _first_step():
        yb_scr[...] = jnp.zeros_like(yb_scr)
        qq = lax.broadcasted_iota(jnp.int32, (sb, win), 0)
        kk = lax.broadcasted_iota(jnp.int32, (sb, win), 1)
        if band:
            dc = (kk // CHUNK) - (qq // CHUNK)
            ok = jnp.logical_and(dc >= 0, dc <= BAND_ROWS // CHUNK)
        else:
            ok = kk < past + CHUNK
        for h in range(N_HEADS):
            base = jnp.broadcast_to(row0_ref[h:h + 1, :], (sb, ROLL_W))
            toep = pltpu.roll(base, 0, 1, stride=1, stride_axis=0)[:, :win] * LOG2E
            bias_scr[h] = jnp.where(ok, toep, NEG_INF)
        bias_scr[N_HEADS] = jnp.full((sb, win), NEG_INF, _F32)

    def rows(prev_ref, cur_ref, so, cols):
        parts = []
        if so < past:
            parts.append(prev_ref[so:past, cols])
        parts.append(cur_ref[max(so - past, 0):so + sb, cols])
        return parts[0] if len(parts) == 1 else jnp.concatenate(parts, axis=0)

    def scores(n, s, h):
        cols = slice(h * HEAD_DIM, (h + 1) * HEAD_DIM)
        qh = q_ref[s * sb:(s + 1) * sb, cols]
        kh = rows(kp_ref, kc_ref, s * sb, cols)
        sc = lax.dot_general(qh, kh, (((1,), (1,)), ((), ())), preferred_element_type=_F32)
        before_start = past - s * sb if band else 0
        biased = []
        for c0 in range(0, win, HEAD_DIM):
            lanes = slice(c0, c0 + HEAD_DIM)
            slot = jnp.where(i == 0, N_HEADS, h) if c0 + HEAD_DIM <= before_start else h
            biased.append(sc[:, lanes] + bias_scr[slot, :, lanes])
        sc_scr[n % SC_SLOTS] = jnp.concatenate(biased, axis=1)

    def probs(n, s, h):
        sc = sc_scr[n % SC_SLOTS]
        return jnp.exp2(sc - jnp.max(sc, axis=-1, keepdims=True)).astype(_BF16)

    ones_cols = jnp.ones((win, HEAD_DIM), _BF16)

    def weighted_values(s, h, p):
        cols = slice(h * HEAD_DIM, (h + 1) * HEAD_DIM)
        vh = jnp.concatenate([rows(vp_ref, vc_ref, s * sb, cols), ones_cols], axis=1)
        ol = jnp.dot(p, vh, preferred_element_type=_F32)
        gate = sgb_ref[s * sb:(s + 1) * sb, cols].astype(_F32)
        scale = gate * (1.0 / ol[:, HEAD_DIM:])
        yb_scr[slot_w, s * sb:(s + 1) * sb, cols] = (ol[:, :HEAD_DIM] * scale).astype(_BF16)

    def attention_stages():
        pairs = [(s, h) for s in range(nsub) for h in range(N_HEADS)]
        for n in range(min(SC_AHEAD, len(pairs))):
            scores(n, *pairs[n])
        ready = probs(0, *pairs[0])
        for n, (s, h) in enumerate(pairs):
            yield
            if n + SC_AHEAD < len(pairs):
                scores(n + SC_AHEAD, *pairs[n + SC_AHEAD])
            current = ready
            if n + 1 < len(pairs):
                ready = probs(n + 1, *pairs[n + 1])
            weighted_values(s, h, current)

    g1 = mod_ref[:, 2 * D:3 * D]
    sh2 = mod_ref[:, 3 * D:4 * D]
    sc2 = mod_ref[:, 4 * D:5 * D]
    g2 = mod_ref[:, 5 * D:6 * D]

    def token_stages(rows):
        yb = yb_scr[slot_r, rows, :]
        merged = (ya_ref[rows, :].astype(_F32) + yb.astype(_F32)).astype(_BF16)
        yield
        o = jnp.dot(merged, wo_ref[...], preferred_element_type=_F32)
        yield
        x1 = _layer_norm(ALPHA * x_ref[rows, :] + (1.0 + g1) * o, ln_ref[0:1, :], ln_ref[1:2, :])
        u2 = (x1 * (1.0 + sc2) + sh2).astype(_BF16)
        yield
        f = bd_ref[...]
        for c in range(D_FF // FF_CHUNK):
            cols = slice(c * FF_CHUNK, (c + 1) * FF_CHUNK)
            hid = jnp.dot(u2, wu_ref[:, cols], preferred_element_type=_F32) + bu_ref[:, cols]
            hid = jnp.square(jnp.maximum(hid, 0.0)).astype(_BF16)
            yield
            f = f + jnp.dot(hid, wd_ref[cols, :], preferred_element_type=_F32)
            yield
        y_ref[rows, :] = _layer_norm(ALPHA * x1 + (1.0 + g2) * f, ln_ref[2:3, :], ln_ref[3:4, :])

    def staggered(gens):
        live = list(range(len(gens)))
        tick = 0
        while True:
            for k in list(live):
                if tick >= k and next(gens[k], _DONE) is _DONE:
                    live.remove(k)
            tick += 1
            if not live:
                return
            yield

    parts = TOKEN_GROUP_PARTS if tm >= sum(TOKEN_GROUP_PARTS) * TOKEN_MIN_PART_ROWS else (1,)
    unit = tm // sum(parts)
    bounds = [unit * sum(parts[:k]) for k in range(len(parts) + 1)]
    tokens = staggered([token_stages(slice(bounds[k], bounds[k + 1]))
                        for k in range(len(parts))])
    attention = attention_stages()
    n_att = nsub * N_HEADS + 1
    n_tok = len(parts) - 1 + 4 + 2 * (D_FF // FF_CHUNK)
    done_att = done_tok = 0
    while done_att < n_att or done_tok < n_tok:
        if done_tok == n_tok or (done_att < n_att and done_att * n_tok <= done_tok * n_att):
            next(attention, _DONE)
            done_att += 1
        else:
            next(tokens, _DONE)
            done_tok += 1
    assert next(attention, _DONE) is _DONE and next(tokens, _DONE) is _DONE


def _attn_out(z, kp_arr, vp_arr, sample_cache, row0, x2, ya, mod3, w_out, w_up, b_up, w_down,
              b_down, ln, bsz, t, tm, sb, past, band):
    nt = t // tm
    n_tiles = bsz * nt
    win = past + sb
    kern = functools.partial(_attn_out_kernel, tm=tm, sb=sb, band=band, nt=nt)

    def att_tile(j):
        return jnp.minimum(j, n_tiles - 1)

    def tok_tile(j):
        return jnp.maximum(j - 1, 0)

    def prev_rows(col):
        if sample_cache:
            return lambda j: (att_tile(j), 0)
        return lambda j: (jnp.where(att_tile(j) % nt == 0, att_tile(j), att_tile(j) - 1), col)

    return pl.pallas_call(
        kern,
        grid=(n_tiles + 1,),
        in_specs=[
            pl.BlockSpec((tm, D), lambda j: (att_tile(j), Z_Q)),
            pl.BlockSpec((past, D), prev_rows(Z_K)),
            pl.BlockSpec((tm, D), lambda j: (att_tile(j), Z_K)),
            pl.BlockSpec((past, D), prev_rows(Z_V)),
            pl.BlockSpec((tm, D), lambda j: (att_tile(j), Z_V)),
            pl.BlockSpec((tm, D), lambda j: (att_tile(j), Z_GB)),
            _resident((N_HEADS, ROLL_W)),
            pl.BlockSpec((tm, D), lambda j: (tok_tile(j), 0)),
            pl.BlockSpec((tm, D), lambda j: (tok_tile(j), 0)),
            pl.BlockSpec((None, 1, 6 * D), lambda j: (tok_tile(j) // nt, 0, 0)),
            _resident((D, D)),
            _resident((D, D_FF)),
            _resident((1, D_FF)),
            _resident((D_FF, D)),
            _resident((1, D)),
            _resident((4, D)),
        ],
        out_specs=pl.BlockSpec((tm, D), lambda j: (tok_tile(j), 0)),
        out_shape=jax.ShapeDtypeStruct((bsz * t, D), _F32),
        scratch_shapes=[
            pltpu.VMEM((N_HEADS + 1, sb, win), _F32),
            pltpu.VMEM((SC_SLOTS, sb, win), _F32),
            pltpu.VMEM((2, tm, D), _BF16),
        ],
        compiler_params=pltpu.CompilerParams(
            dimension_semantics=("arbitrary",), vmem_limit_bytes=VMEM_LIMIT_ATTN_OUT),
        name="attn_out",
    )(z, kp_arr, z, vp_arr, z, z, row0, x2, ya, mod3, w_out, w_up, b_up, w_down, b_down, ln)


def _rel_bias_row0(table, past):
    far = jnp.broadcast_to(table[:, 2 * MAX_REL:], (N_HEADS, past - MAX_REL + 1))
    near = table[:, 2 * MAX_REL - 1::-1]
    tail = jnp.broadcast_to(table[:, 2 * MAX_REL:],
                            (N_HEADS, ROLL_W - (past + MAX_REL + 1)))
    return jnp.concatenate([far, near, tail], axis=1).astype(_F32)


def _heads(a, bsz, rows):
    return a.reshape(1, bsz, rows, N_HEADS, HEAD_DIM)


@jax.jit
def _forward(x_prompt, x_sample, c_prompt, c_sample, cache_k, cache_v, state_conv, state_lru,
             w_ada, b_ada, w_in, conv_w, conv_b, w_rg, b_rg, w_ig, b_ig, lru_lambda, rel_bias,
             w_out, ln1_g, ln1_b, w_up, b_up, w_down, b_down, ln2_g, ln2_b):
    bp, t, _ = x_prompt.shape
    bs, s, _ = x_sample.shape
    l = 0
    col_scale = jnp.where(jnp.arange(D_IN) >= COL_GA * D, 0.5, 1.0).astype(_F32)
    w_in_b = (w_in[l] * col_scale[None, :]).astype(_BF16)
    w_out_b = w_out[l].astype(_BF16)
    w_up_b = w_up[l].astype(_BF16)
    w_down_b = w_down[l].astype(_BF16)
    wg = (0.5 * jnp.concatenate([w_rg[l], w_ig[l]], axis=-1)).astype(_BF16)
    bg = 0.5 * jnp.concatenate([b_rg[l], b_ig[l]], axis=-1)[:, None, :]
    lam = lru_lambda[l][None, :]
    cb = conv_b[l][None, :]
    ln = jnp.stack([ln1_g[l], ln1_b[l], ln2_g[l], ln2_b[l]])
    bu = b_up[l][None, :]
    bd = b_down[l][None, :]

    mod = _ada(jnp.concatenate([c_prompt, c_sample], axis=0), w_ada[l].astype(_BF16),
               b_ada[l][None, :])
    mod_p = mod[:bp, None, :]
    mod_s = mod[bp:, None, :]

    tm = 512
    xp2 = x_prompt.reshape(bp * t, D)
    z, ya, conv_p, h_p = _mix_in(xp2, mod_p, w_in_b, jnp.zeros((bp, CONV_W - 1, D), _F32),
                                 jnp.zeros((bp, 1, D), _F32), conv_w[l], cb, wg, bg, lam,
                                 bp, t, tm // 2, True)
    y_p = _attn_out(z, z, z, False, _rel_bias_row0(rel_bias[l], BAND_ROWS), xp2, ya, mod_p,
                    w_out_b, w_up_b, bu, w_down_b, bd, ln, bp, t, tm, 2 * CHUNK, BAND_ROWS, True)
    z3 = z.reshape(bp, t, D_Z)
    rows = min(BAND_ROWS, t)
    k_p = z3[:, t - rows:, Z_K * D:(Z_K + 1) * D].astype(_F32)
    v_p = z3[:, t - rows:, Z_V * D:(Z_V + 1) * D].astype(_F32)

    xs2 = x_sample.reshape(bs * s, D)
    zs, ya_s, conv_s, h_s = _mix_in(xs2, mod_s, w_in_b, state_conv[l], state_lru[l][:, None, :],
                                    conv_w[l], cb, wg, bg, lam, bs, s, s, False)
    sbs = 2 * CHUNK
    zs3 = zs.reshape(bs, s, D_Z)

    def pad_rows(a):
        a3 = a.reshape(bs, s, a.shape[-1])
        return jnp.pad(a3, ((0, 0), (0, sbs - s), (0, 0))).reshape(bs * sbs, a.shape[-1])

    n_past = cache_k.shape[2]
    ck = cache_k[l].reshape(bs * n_past, D).astype(_BF16)
    cv = cache_v[l].reshape(bs * n_past, D).astype(_BF16)
    y_s = _attn_out(pad_rows(zs), ck, cv, True, _rel_bias_row0(rel_bias[l], n_past),
                    pad_rows(xs2), pad_rows(ya_s), mod_s, w_out_b, w_up_b, bu, w_down_b, bd, ln,
                    bs, sbs, sbs, sbs, n_past, False)
    y_s = y_s.reshape(bs, sbs, D)[:, :s]
    k_s = zs3[:, :, Z_K * D:(Z_K + 1) * D].astype(_F32)
    v_s = zs3[:, :, Z_V * D:(Z_V + 1) * D].astype(_F32)

    return (y_p.reshape(bp, t, D), y_s.reshape(bs, s, D),
            _heads(k_p, bp, rows), _heads(v_p, bp, rows), conv_p[None], h_p.reshape(1, bp, D),
            _heads(k_s, bs, s), _heads(v_s, bs, s), conv_s[None], h_s.reshape(1, bs, D))


def kernel(x_prompt, x_sample, c_prompt, c_sample, cache_k, cache_v, state_conv, state_lru, w_ada, b_ada, w_in, conv_w, conv_b, w_rg, b_rg, w_ig, b_ig, lru_lambda, rel_bias, w_out, ln1_g, ln1_b, w_up, b_up, w_down, b_down, ln2_g, ln2_b):
    return _forward(x_prompt, x_sample, c_prompt, c_sample, cache_k, cache_v, state_conv,
                    state_lru, w_ada, b_ada, w_in, conv_w, conv_b, w_rg, b_rg, w_ig, b_ig,
                    lru_lambda, rel_bias, w_out, ln1_g, ln1_b, w_up, b_up, w_down, b_down,
                    ln2_g, ln2_b)
```
